```python
import jax, jax.numpy as jnp
from jax import lax
import numpy as np

D_MODEL = 2048
BATCH = 1
SEQ = 8192
DEPTH = 2
DEC_BATCH = 128
DEC_SEQ = 4
PAST_LEN = 2048
PAGE_SIZE = 128

N_MIXERS = 2
N_SB_LAYERS = (DEPTH + 1) // 2
N_SSD_LAYERS = DEPTH // 2
SB_HEADS = 16
SB_HEAD_DIM = D_MODEL // SB_HEADS
SB_WIDTH = SB_HEADS * SB_HEAD_DIM
SB_BLOCK = 128
SB_BIAS_INIT = -8.0
SSD_INNER = 2 * D_MODEL
SSD_HEAD_DIM = 64
SSD_HEADS = SSD_INNER // SSD_HEAD_DIM
SSD_GROUPS = 8
SSD_STATE = 128
SSD_CONV = 4
SSD_CHUNK = 128
SSD_CONV_DIM = SSD_INNER + 2 * SSD_GROUPS * SSD_STATE
SSD_IN = 2 * SSD_INNER + 2 * SSD_GROUPS * SSD_STATE + SSD_HEADS
RMS_EPS = 1e-6

kernel_name = "stickbreak_mamba2_hybrid_step"


def rmsnorm(x, g):
    xf = x.astype(jnp.float32)
    y = xf * lax.rsqrt(jnp.mean(xf * xf, axis=-1, keepdims=True) + RMS_EPS)
    return (y * g.astype(jnp.float32)).astype(x.dtype)


def stick_breaking(q, keys, values, q_pos, k_pos, bias):
    scale = SB_HEAD_DIM ** -0.5
    z = jnp.concatenate([jnp.einsum('bqhd,bkhd->bhqk', q, k).astype(jnp.float32) for k in keys], axis=-1) * scale
    z = z + bias.astype(jnp.float32)[None, :, None, None]
    mask = k_pos[None, :] < q_pos[:, None]
    log_beta = jax.nn.log_sigmoid(z)
    log_keep = jnp.where(mask, jax.nn.log_sigmoid(-z), 0.0)
    later = lax.cumsum(log_keep, axis=3, reverse=True) - log_keep
    w = jnp.where(mask, jnp.exp(log_beta + later), 0.0)
    offsets = np.cumsum([k.shape[1] for k in keys])[:-1].tolist()
    w_parts = jnp.split(w, offsets, axis=-1)
    o = jnp.einsum('bhqk,bkhd->bqhd', w_parts[0], values[0].astype(jnp.float32))
    for wp, vp in zip(w_parts[1:], values[1:]):
        o = o + jnp.einsum('bhqk,bkhd->bqhd', wp, vp.astype(jnp.float32))
    return o


def sb_mixer(h, w_in, w_out, bias, k_past, v_past):
    b, l, _ = h.shape
    q, k, v, gate = jnp.split(h @ w_in, 4, axis=-1)
    q = q.reshape(b, l, SB_HEADS, SB_HEAD_DIM)
    k = k.reshape(b, l, SB_HEADS, SB_HEAD_DIM)
    v = v.reshape(b, l, SB_HEADS, SB_HEAD_DIM)
    if k_past is None:
        nb = l // SB_BLOCK
        k_pos = jnp.arange(l)
        qb = jnp.moveaxis(q.reshape(b, nb, SB_BLOCK, SB_HEADS, SB_HEAD_DIM), 1, 0)

        def block(args):
            q_blk, i = args
            return stick_breaking(q_blk, (k,), (v,), i * SB_BLOCK + jnp.arange(SB_BLOCK), k_pos, bias)

        o = lax.map(block, (qb, jnp.arange(nb)))
        o = jnp.moveaxis(o, 0, 1).reshape(b, l, SB_HEADS, SB_HEAD_DIM)
    else:
        past = k_past.shape[1]
        o = stick_breaking(q, (k_past, k), (v_past, v), past + jnp.arange(l), jnp.arange(past + l), bias)
    o = o.reshape(b, l, SB_WIDTH).astype(h.dtype) * jax.nn.silu(gate)
    return o @ w_out, k, v


def causal_conv(xbc, conv_state, w, bias):
    l = xbc.shape[1]
    xp = jnp.concatenate([conv_state.astype(xbc.dtype), xbc], axis=1)
    out = bias
    for j in range(SSD_CONV):
        out = out + xp[:, j:j + l] * w[j]
    return jax.nn.silu(out), xp[:, l:]


def ssd_chunked(xs, dt, a, bm, cm, s0, chunk):
    f32 = jnp.float32
    b, l, h, p = xs.shape
    g, n = bm.shape[2], bm.shape[3]
    hg = h // g
    c = l // chunk
    xdt = (xs.astype(f32) * dt[..., None]).reshape(b, c, chunk, g, hg, p)
    acum = jnp.cumsum((dt * a).reshape(b, c, chunk, g, hg), axis=2)
    bq = bm.astype(f32).reshape(b, c, chunk, g, n)
    cq = cm.astype(f32).reshape(b, c, chunk, g, n)
    causal = jnp.tril(jnp.ones((chunk, chunk), bool))[:, :, None, None]
    seg = acum[:, :, :, None] - acum[:, :, None, :]
    decay = jnp.exp(jnp.where(causal, seg, -jnp.inf))
    cb = jnp.einsum('bcqgn,bcsgn->bcqsg', cq, bq)
    y_intra = jnp.einsum('bcqsg,bcqsgh,bcsghp->bcqghp', cb, decay, xdt)
    to_end = jnp.exp(acum[:, :, -1:] - acum)
    chunk_states = jnp.einsum('bcsgn,bcsgh,bcsghp->bcghpn', bq, to_end, xdt)
    chunk_decay = jnp.exp(acum[:, :, -1])

    def step(state, inp):
        st, dec = inp
        return state * dec[..., None, None] + st, state

    final, s_in = lax.scan(step, s0.astype(f32).reshape(b, g, hg, p, n),
                           (jnp.moveaxis(chunk_states, 1, 0), jnp.moveaxis(chunk_decay, 1, 0)))
    s_in = jnp.moveaxis(s_in, 0, 1)
    y_inter = jnp.einsum('bcqgn,bcqgh,bcghpn->bcqghp', cq, jnp.exp(acum), s_in)
    return (y_intra + y_inter).reshape(b, l, h, p), final.reshape(b, h, p, n)


def ssd_mixer(h, conv_state, ssm_state, w_in, conv_w, conv_b, dt_bias, a_log, d_skip, gnorm_w, w_out):
    f32 = jnp.float32
    b, l, _ = h.shape
    z, xbc, dt_raw = jnp.split(h @ w_in, [SSD_INNER, SSD_INNER + SSD_CONV_DIM], axis=-1)
    xbc, new_conv = causal_conv(xbc, conv_state, conv_w, conv_b)
    xs, bm, cm = jnp.split(xbc, [SSD_INNER, SSD_INNER + SSD_GROUPS * SSD_STATE], axis=-1)
    xs = xs.reshape(b, l, SSD_HEADS, SSD_HEAD_DIM)
    bm = bm.reshape(b, l, SSD_GROUPS, SSD_STATE)
    cm = cm.reshape(b, l, SSD_GROUPS, SSD_STATE)
    dt = jax.nn.softplus(dt_raw.astype(f32) + dt_bias.astype(f32))
    a = -jnp.exp(a_log.astype(f32))
    chunk = SSD_CHUNK if l % SSD_CHUNK == 0 else l
    y, new_ssm = ssd_chunked(xs, dt, a, bm, cm, ssm_state, chunk)
    y = y + d_skip.astype(f32)[:, None] * xs.astype(f32)
    y = y.reshape(b, l, SSD_INNER) * jax.nn.silu(z.astype(f32))
    yg = y.reshape(b, l, SSD_GROUPS, SSD_INNER // SSD_GROUPS)
    yg = yg * lax.rsqrt(jnp.mean(yg * yg, axis=-1, keepdims=True) + RMS_EPS)
    y = (yg.reshape(b, l, SSD_INNER) * gnorm_w.astype(f32)).astype(h.dtype)
    return y @ w_out, new_conv, new_ssm.astype(ssm_state.dtype)


def trunk(x, past_kv, conv_states, ssm_states, norm_sb, w_in_sb, w_out_sb, sb_bias, norm_ssd, w_in_ssd, conv_w,
          conv_b, dt_bias, a_log, d_skip, gnorm_w, w_out_ssd, norm_f):
    ks, vs, convs, ssms = [], [], [], []
    for i in range(DEPTH):
        j = i // N_MIXERS
        if i % N_MIXERS == 0:
            kp, vp = (None, None) if past_kv is None else past_kv[j]
            out, k, v = sb_mixer(rmsnorm(x, norm_sb[j]), w_in_sb[j], w_out_sb[j], sb_bias[j], kp, vp)
            ks.append(k)
            vs.append(v)
        else:
            out, cst, sst = ssd_mixer(rmsnorm(x, norm_ssd[j]), conv_states[j], ssm_states[j], w_in_ssd[j], conv_w[j],
                                      conv_b[j], dt_bias[j], a_log[j], d_skip[j], gnorm_w[j], w_out_ssd[j])
            convs.append(cst)
            ssms.append(sst)
        x = x + out
    return rmsnorm(x, norm_f), jnp.stack(ks), jnp.stack(vs), jnp.stack(convs), jnp.stack(ssms)


def setup_inputs(seed: int = 0) -> dict:
    key = jax.random.key(seed)
    ks = jax.random.split(key, 24)
    f32 = jnp.float32
    n_pages = PAST_LEN // PAGE_SIZE
    n_used = DEC_BATCH * n_pages
    n_phys = n_used + max(1, n_used // 4)
    page_table = jax.random.permutation(ks[0], n_phys)[:n_used].reshape(DEC_BATCH, n_pages).astype(jnp.int32)
    dt0 = jnp.exp(jax.random.uniform(ks[1], (N_SSD_LAYERS, SSD_HEADS), f32, np.log(1e-3), np.log(1e-1)))
    dt_bias = dt0 + jnp.log(-jnp.expm1(-dt0))
    return {
        "x_prompt": jax.random.normal(ks[2], (BATCH, SEQ, D_MODEL), f32),
        "x_sample": jax.random.normal(ks[3], (DEC_BATCH, DEC_SEQ, D_MODEL), f32),
        "cache_k": jax.random.normal(ks[4], (N_SB_LAYERS, n_phys, PAGE_SIZE, SB_HEADS, SB_HEAD_DIM), f32),
        "cache_v": jax.random.normal(ks[5], (N_SB_LAYERS, n_phys, PAGE_SIZE, SB_HEADS, SB_HEAD_DIM), f32),
        "state_conv": jax.random.normal(ks[6], (N_SSD_LAYERS, DEC_BATCH, SSD_CONV - 1, SSD_CONV_DIM), f32),
        "state_ssm": 0.1 * jax.random.normal(ks[7], (N_SSD_LAYERS, DEC_BATCH, SSD_HEADS, SSD_HEAD_DIM, SSD_STATE), f32),
        "page_table": page_table,
        "norm_sb": 1.0 + 0.02 * jax.random.normal(ks[8], (N_SB_LAYERS, D_MODEL), f32),
        "w_in_sb": jax.random.normal(ks[9], (N_SB_LAYERS, D_MODEL, 4 * SB_WIDTH), f32) * D_MODEL ** -0.5,
        "w_out_sb": jax.random.normal(ks[10], (N_SB_LAYERS, SB_WIDTH, D_MODEL), f32) * SB_WIDTH ** -0.5,
        "sb_bias": SB_BIAS_INIT + 0.5 * jax.random.normal(ks[20], (N_SB_LAYERS, SB_HEADS), f32),
        "norm_ssd": 1.0 + 0.02 * jax.random.normal(ks[11], (N_SSD_LAYERS, D_MODEL), f32),
        "w_in_ssd": jax.random.normal(ks[12], (N_SSD_LAYERS, D_MODEL, SSD_IN), f32) * D_MODEL ** -0.5,
        "conv_w": jax.random.normal(ks[13], (N_SSD_LAYERS, SSD_CONV, SSD_CONV_DIM), f32) * SSD_CONV ** -0.5,
        "conv_b": 0.02 * jax.random.normal(ks[14], (N_SSD_LAYERS, SSD_CONV_DIM), f32),
        "dt_bias": dt_bias,
        "a_log": jnp.log(jax.random.uniform(ks[15], (N_SSD_LAYERS, SSD_HEADS), f32, 1.0, 16.0)),
        "d_skip": 1.0 + 0.02 * jax.random.normal(ks[16], (N_SSD_LAYERS, SSD_HEADS), f32),
        "gnorm_w": 1.0 + 0.02 * jax.random.normal(ks[17], (N_SSD_LAYERS, SSD_INNER), f32),
        "w_out_ssd": jax.random.normal(ks[18], (N_SSD_LAYERS, SSD_INNER, D_MODEL), f32) * SSD_INNER ** -0.5,
        "norm_f": 1.0 + 0.02 * jax.random.normal(ks[19], (D_MODEL,), f32),
    }


def reference(x_prompt, x_sample, cache_k, cache_v, state_conv, state_ssm, page_table, norm_sb, w_in_sb, w_out_sb,
              sb_bias, norm_ssd, w_in_ssd, conv_w, conv_b, dt_bias, a_log, d_skip, gnorm_w, w_out_ssd, norm_f):
    zero_conv = jnp.zeros((N_SSD_LAYERS, x_prompt.shape[0], SSD_CONV - 1, SSD_CONV_DIM), x_prompt.dtype)
    zero_ssm = jnp.zeros((N_SSD_LAYERS, x_prompt.shape[0], SSD_HEADS, SSD_HEAD_DIM, SSD_STATE), x_prompt.dtype)
    y_prompt, k_prompt, v_prompt, conv_prompt, ssm_prompt = trunk(
        x_prompt, None, zero_conv, zero_ssm, norm_sb, w_in_sb, w_out_sb, sb_bias, norm_ssd, w_in_ssd, conv_w, conv_b,
        dt_bias, a_log, d_skip, gnorm_w, w_out_ssd, norm_f)
    n_seq, n_pages = page_table.shape
    past_len = n_pages * PAGE_SIZE
    past_kv = [(cache_k[j][page_table].reshape(n_seq, past_len, SB_HEADS, SB_HEAD_DIM),
                cache_v[j][page_table].reshape(n_seq, past_len, SB_HEADS, SB_HEAD_DIM))
               for j in range(N_SB_LAYERS)]
    y_sample, k_sample, v_sample, conv_sample, ssm_sample = trunk(
        x_sample, past_kv, state_conv, state_ssm, norm_sb, w_in_sb, w_out_sb, sb_bias, norm_ssd, w_in_ssd, conv_w,
        conv_b, dt_bias, a_log, d_skip, gnorm_w, w_out_ssd, norm_f)
    return (y_prompt, y_sample, k_prompt, v_prompt, conv_prompt, ssm_prompt, k_sample, v_sample, conv_sample, ssm_sample)
```

```python
import functools

import jax
import jax.numpy as jnp
from jax import lax
from jax.experimental import pallas as pl
from jax.experimental.pallas import tpu as pltpu

F32 = jnp.float32
BF16 = jnp.bfloat16
RMS_EPS = 1e-6

SB_HEAD_DIM = 128
SSD_HEAD_DIM = 64
SSD_GROUPS = 8
SSD_STATE = 128
SSD_CONV = 4
SSD_CHUNK = 128

LANES = 128
SUBLANES = 8
VMEM_PHYSICAL_BYTES = 64 * 1024 * 1024
VMEM_LIMIT_CAP_BYTES = 56 * 1024 * 1024


def _vmem_limit(block_bytes, scratch_bytes=0, temp_bytes=0):
    est = 2 * block_bytes + scratch_bytes + temp_bytes
    return int(min(max(est, 16 * 1024 * 1024), VMEM_LIMIT_CAP_BYTES))


def _nbytes(shape, dtype):
    n = 1
    for s in shape:
        n *= s
    return n * jnp.dtype(dtype).itemsize


def _sigmoid(x):
    return 1.0 / (1.0 + jnp.exp(-x))


def _softplus(x):
    return jnp.maximum(x, 0.0) + jnp.log(1.0 + jnp.exp(-jnp.abs(x)))


def _split2(x):
    hi = x.astype(BF16)
    lo = (x - hi.astype(F32)).astype(BF16)
    return hi, lo


def _split3(x):
    hi = x.astype(BF16)
    r1 = x - hi.astype(F32)
    mid = r1.astype(BF16)
    lo = (r1 - mid.astype(F32)).astype(BF16)
    return hi, mid, lo


def _dot(a, b):
    return jnp.dot(a, b, preferred_element_type=F32)


def _dot_nt(a, b):
    return lax.dot_general(a, b, (((1,), (1,)), ((), ())), preferred_element_type=F32)


def _dot_tn(a, b):
    return lax.dot_general(a, b, (((0,), (0,)), ((), ())), preferred_element_type=F32)


def _norm_matmul_kernel(x_ref, g_ref, w_ref, o_ref, xn_ref, *, out_scale):
    @pl.when(pl.program_id(1) == 0)
    def _():
        x = x_ref[...]
        ms = jnp.mean(x * x, axis=-1, keepdims=True)
        xn_ref[...] = (x * lax.rsqrt(ms + RMS_EPS) * g_ref[...]).astype(BF16)

    acc = _dot(xn_ref[...], w_ref[...])
    if out_scale != 1.0:
        acc = acc * out_scale
    o_ref[...] = acc.astype(o_ref.dtype)


def norm_matmul(x, g, w, *, col_start, n_cols, out_dtype, out_scale=1.0, bm, bn):
    m, k = x.shape
    assert m % bm == 0 and n_cols % bn == 0 and col_start % bn == 0
    off = col_start // bn
    blocks = _nbytes((bm, k), F32) + _nbytes((k, bn), BF16) + _nbytes((bm, bn), out_dtype)
    return pl.pallas_call(
        functools.partial(_norm_matmul_kernel, out_scale=out_scale),
        grid=(m // bm, n_cols // bn),
        in_specs=[
            pl.BlockSpec((bm, k), lambda i, j: (i, 0)),
            pl.BlockSpec((1, k), lambda i, j: (0, 0)),
            pl.BlockSpec((k, bn), lambda i, j: (0, j + off)),
        ],
        out_specs=pl.BlockSpec((bm, bn), lambda i, j: (i, j)),
        out_shape=jax.ShapeDtypeStruct((m, n_cols), out_dtype),
        scratch_shapes=[pltpu.VMEM((bm, k), BF16)],
        compiler_params=pltpu.CompilerParams(
            dimension_semantics=("arbitrary", "arbitrary"),
            vmem_limit_bytes=_vmem_limit(blocks, _nbytes((bm, k), BF16), 2 * _nbytes((bm, k), F32)),
        ),
        name="norm_matmul",
    )(x, g.reshape(1, k), w)


def _matmul_res_kernel(a_ref, w_ref, r_ref, g_ref, o_ref, *, final_norm):
    kk = pl.program_id(1)

    @pl.when(kk == 0)
    def _():
        o_ref[...] = r_ref[...]

    o_ref[...] += _dot(a_ref[...].astype(BF16), w_ref[...])

    if final_norm:
        @pl.when(kk == pl.num_programs(1) - 1)
        def _():
            x = o_ref[...]
            ms = jnp.mean(x * x, axis=-1, keepdims=True)
            o_ref[...] = x * lax.rsqrt(ms + RMS_EPS) * g_ref[...]


def matmul_res(a, w, res, g=None, *, bm, bk):
    m, k = a.shape
    n = w.shape[1]
    assert m % bm == 0 and k % bk == 0
    final_norm = g is not None
    if g is None:
        g = jnp.ones((n,), F32)
    blocks = _nbytes((bm, bk), a.dtype) + _nbytes((bk, n), BF16) + 2 * _nbytes((bm, n), F32)
    return pl.pallas_call(
        functools.partial(_matmul_res_kernel, final_norm=final_norm),
        grid=(m // bm, k // bk),
        in_specs=[
            pl.BlockSpec((bm, bk), lambda i, j: (i, j)),
            pl.BlockSpec((bk, n), lambda i, j: (j, 0)),
            pl.BlockSpec((bm, n), lambda i, j: (i, 0)),
            pl.BlockSpec((1, n), lambda i, j: (0, 0)),
        ],
        out_specs=pl.BlockSpec((bm, n), lambda i, j: (i, 0)),
        out_shape=jax.ShapeDtypeStruct((m, n), F32),
        compiler_params=pltpu.CompilerParams(
            dimension_semantics=("arbitrary", "arbitrary"),
            vmem_limit_bytes=_vmem_limit(blocks, 0, 2 * _nbytes((bm, n), F32)),
        ),
        name="matmul_res",
    )(a, w, res, g.reshape(1, n))


def _neg_softplus(z):
    return -_softplus(z)


def _sb_prompt_kernel(bias_ref, q_ref, k_ref, v_ref, gate_ref, o_ref, *, tq, tk):
    h = pl.program_id(0)
    qi = pl.program_id(1)
    bias = bias_ref[h]
    q = q_ref[...]
    nd = tq // tk

    r = lax.broadcasted_iota(jnp.int32, (2 * tk, tk), 0)
    c = lax.broadcasted_iota(jnp.int32, (2 * tk, tk), 1)
    tri2 = jnp.where(jnp.where(r >= tk, r - tk, r) >= c, 1.0, 0.0).astype(BF16)

    def step(kb, carry, acc, mask):
        start = pl.multiple_of(kb * tk, tk)
        kblk = k_ref[pl.ds(start, tk), :].astype(BF16)
        vblk = v_ref[pl.ds(start, tk), :].astype(BF16)
        z = _dot_nt(q, kblk) + bias
        lk = _neg_softplus(z)
        if mask is not None:
            lk = jnp.where(mask, lk, 0.0)
        hi, lo = _split2(lk)
        cs = _dot(jnp.concatenate([hi, lo], axis=1), tri2) + carry
        w = jnp.exp(z + cs)
        if mask is not None:
            w = jnp.where(mask, w, 0.0)
        acc = acc + _dot(w.astype(BF16), vblk)
        return cs[:, 0:1], acc

    carry = jnp.zeros((tq, 1), F32)
    acc = jnp.zeros((tq, SB_HEAD_DIM), F32)
    t_pos = qi * tq + lax.broadcasted_iota(jnp.int32, (tq, tk), 0)
    s_loc = lax.broadcasted_iota(jnp.int32, (tq, tk), 1)
    for d in range(nd):
        kb = qi * nd + (nd - 1 - d)
        mask = (kb * tk + s_loc) < t_pos
        carry, acc = step(kb, carry, acc, mask)

    def body(i, ca):
        return step(qi * nd - 1 - i, ca[0], ca[1], None)

    carry, acc = lax.fori_loop(0, qi * nd, body, (carry, acc))
    g = gate_ref[...]
    o_ref[...] = (acc * (g * _sigmoid(g))).astype(o_ref.dtype)


def sb_prompt_attn(q, k, v, gate, bias, *, tq, tk):
    l, width = q.shape
    d = SB_HEAD_DIM
    nh = width // d
    assert l % tq == 0 and tq % tk == 0
    blocks = _nbytes((tq, d), BF16) * 2 + 2 * _nbytes((l, d), F32) + _nbytes((tq, d), F32)
    return pl.pallas_call(
        functools.partial(_sb_prompt_kernel, tq=tq, tk=tk),
        grid=(nh, l // tq),
        in_specs=[
            pl.BlockSpec(memory_space=pltpu.SMEM),
            pl.BlockSpec((tq, d), lambda h, i: (i, h)),
            pl.BlockSpec((l, d), lambda h, i: (0, h)),
            pl.BlockSpec((l, d), lambda h, i: (0, h)),
            pl.BlockSpec((tq, d), lambda h, i: (i, h)),
        ],
        out_specs=pl.BlockSpec((tq, d), lambda h, i: (i, h)),
        out_shape=jax.ShapeDtypeStruct((l, width), BF16),
        compiler_params=pltpu.CompilerParams(
            dimension_semantics=("arbitrary", "arbitrary"),
            vmem_limit_bytes=_vmem_limit(blocks, 0, 12 * _nbytes((tq, tk), F32)),
        ),
        name="sb_prompt_attn",
    )(bias, q, k, v, gate)


def _sb_paged_kernel(pt_ref, q_ref, kn_ref, vn_ref, gate_ref, bias_ref, kp_ref, vp_ref, o_ref,
                     qrows_ref, kscr_ref, vscr_ref, carry_ref, acc_ref, *, nq, nh, page):
    b = pl.program_id(0)
    p = pl.program_id(1)
    d = SB_HEAD_DIM
    width = nh * d
    ncol = nh * nq

    r = lax.broadcasted_iota(jnp.int32, (page, 2 * page), 0)
    c = lax.broadcasted_iota(jnp.int32, (page, 2 * page), 1)
    tri2 = jnp.where(jnp.where(c >= page, c - page, c) >= r, 1.0, 0.0).astype(BF16)

    def page_step(k_f32, v_f32, mask):
        kb = k_f32.astype(BF16)
        vb = v_f32.astype(BF16)
        z = _dot_nt(kb, qrows_ref[...]) + bias_ref[...]
        lk = _neg_softplus(z)
        if mask is not None:
            lk = jnp.where(mask, lk, 0.0)
        hi, lo = _split2(lk)
        cs = _dot(tri2, jnp.concatenate([hi, lo], axis=0)) + carry_ref[...]
        w = jnp.exp(z + cs)
        if mask is not None:
            w = jnp.where(mask, w, 0.0)
        acc_ref[...] += _dot_tn(w.astype(BF16), vb)
        carry_ref[...] = cs[0:1, :]

    @pl.when((b == 0) & (p == 0))
    def _():
        kscr_ref[...] = jnp.zeros_like(kscr_ref)
        vscr_ref[...] = jnp.zeros_like(vscr_ref)

    @pl.when(p == 0)
    def _():
        qt = jnp.concatenate([q_ref[0]] * nh, axis=0)
        rh = lax.broadcasted_iota(jnp.int32, (ncol, width), 0) // nq
        ch = lax.broadcasted_iota(jnp.int32, (ncol, width), 1) // d
        qrows_ref[...] = jnp.where(rh == ch, qt, 0.0).astype(BF16)
        carry_ref[...] = jnp.zeros_like(carry_ref)
        acc_ref[...] = jnp.zeros_like(acc_ref)
        kscr_ref[0:nq, :] = kn_ref[0]
        vscr_ref[0:nq, :] = vn_ref[0]
        j = lax.broadcasted_iota(jnp.int32, (page, ncol), 0)
        i = lax.broadcasted_iota(jnp.int32, (page, ncol), 1) % nq
        page_step(kscr_ref[...], vscr_ref[...], j < i)

    page_step(kp_ref[...], vp_ref[...], None)

    @pl.when(p == pl.num_programs(1) - 1)
    def _():
        acc = acc_ref[...]
        rh = lax.broadcasted_iota(jnp.int32, (ncol, d), 0) // nq
        o = jnp.zeros((ncol, d), F32)
        for hh in range(nh):
            o = o + jnp.where(rh == hh, acc[:, hh * d:(hh + 1) * d], 0.0)
        o = jnp.concatenate([o[hh * nq:(hh + 1) * nq, :] for hh in range(nh)], axis=1)
        g = gate_ref[0]
        o_ref[0] = o * (g * _sigmoid(g))


def sb_paged_attn(q, k_new, v_new, gate, bias, cache_k, cache_v, page_table, page_base):
    nb, nq, width = q.shape
    d = SB_HEAD_DIM
    nh = width // d
    n_pages = page_table.shape[1]
    page = cache_k.shape[1]
    ncol = nh * nq
    bias_cols = jnp.repeat(bias.astype(F32), nq).reshape(1, ncol)
    pt = page_table.reshape(-1).astype(jnp.int32) + page_base

    def tok_map(b, p, pt_ref):
        return (b, 0, 0)

    def page_map(b, p, pt_ref):
        return (pt_ref[b * n_pages + (n_pages - 1 - p)], 0, 0)

    blocks = 4 * _nbytes((nq, width), F32) + 2 * _nbytes((page, width), F32) + _nbytes((nq, width), F32)
    scratch = (_nbytes((ncol, width), BF16) + 2 * _nbytes((page, width), F32) + _nbytes((ncol, width), F32))
    grid_spec = pltpu.PrefetchScalarGridSpec(
        num_scalar_prefetch=1,
        grid=(nb, n_pages),
        in_specs=[
            pl.BlockSpec((1, nq, width), tok_map),
            pl.BlockSpec((1, nq, width), tok_map),
            pl.BlockSpec((1, nq, width), tok_map),
            pl.BlockSpec((1, nq, width), tok_map),
            pl.BlockSpec((1, ncol), lambda b, p, pt_ref: (0, 0)),
            pl.BlockSpec((None, page, width), page_map),
            pl.BlockSpec((None, page, width), page_map),
        ],
        out_specs=pl.BlockSpec((1, nq, width), tok_map),
        scratch_shapes=[
            pltpu.VMEM((ncol, width), BF16),
            pltpu.VMEM((page, width), F32),
            pltpu.VMEM((page, width), F32),
            pltpu.VMEM((1, ncol), F32),
            pltpu.VMEM((ncol, width), F32),
        ],
    )
    return pl.pallas_call(
        functools.partial(_sb_paged_kernel, nq=nq, nh=nh, page=page),
        grid_spec=grid_spec,
        out_shape=jax.ShapeDtypeStruct((nb, nq, width), F32),
        compiler_params=pltpu.CompilerParams(
            dimension_semantics=("arbitrary", "arbitrary"),
            vmem_limit_bytes=_vmem_limit(blocks, scratch, 4 * _nbytes((page, width), F32)),
        ),
        name="sb_paged_attn",
    )(pt, q, k_new, v_new, gate, bias_cols, cache_k, cache_v)


def _ssd_prep_kernel(dtraw_ref, dtb_ref, alog_ref, e3_ref, dte_ref, ace_ref, ac_ref, *, chunk):
    rows = dtraw_ref.shape[0]
    dt = _softplus(dtraw_ref[...] + dtb_ref[...])
    a = -jnp.exp(alog_ref[...])
    da = dt * a
    q = lax.broadcasted_iota(jnp.int32, (rows, 3 * rows), 0)
    s = lax.broadcasted_iota(jnp.int32, (rows, 3 * rows), 1) % rows
    tri3 = jnp.where((s <= q) & (s // chunk == q // chunk), 1.0, 0.0).astype(BF16)
    acum = _dot(tri3, jnp.concatenate(_split3(da), axis=0))
    ac_ref[...] = acum
    dte_ref[...] = _dot(jnp.concatenate(_split3(dt), axis=1), e3_ref[...])
    ace_ref[...] = _dot(jnp.concatenate(_split3(acum), axis=1), e3_ref[...])


def ssd_prep(dt_raw, dt_bias, a_log, *, chunk, rows):
    m = dt_raw.shape[0]
    nh = dt_bias.shape[0]
    inner = nh * SSD_HEAD_DIM
    assert m % rows == 0 and rows % chunk == 0 and nh <= LANES

    def pad(v):
        return jnp.zeros((1, LANES), F32).at[0, :nh].set(v.astype(F32))

    lane = jnp.arange(3 * LANES) % LANES
    e3 = (lane[:, None] == (jnp.arange(inner) // SSD_HEAD_DIM)[None, :]).astype(BF16)
    blocks = 2 * _nbytes((rows, LANES), F32) + _nbytes((3 * LANES, inner), BF16) + 2 * _nbytes((rows, inner), F32)
    return pl.pallas_call(
        functools.partial(_ssd_prep_kernel, chunk=chunk),
        grid=(m // rows,),
        in_specs=[
            pl.BlockSpec((rows, LANES), lambda i: (i, 0)),
            pl.BlockSpec((1, LANES), lambda i: (0, 0)),
            pl.BlockSpec((1, LANES), lambda i: (0, 0)),
            pl.BlockSpec((3 * LANES, inner), lambda i: (0, 0)),
        ],
        out_specs=[
            pl.BlockSpec((rows, inner), lambda i: (i, 0)),
            pl.BlockSpec((rows, inner), lambda i: (i, 0)),
            pl.BlockSpec((rows, LANES), lambda i: (i, 0)),
        ],
        out_shape=[
            jax.ShapeDtypeStruct((m, inner), F32),
            jax.ShapeDtypeStruct((m, inner), F32),
            jax.ShapeDtypeStruct((m, LANES), F32),
        ],
        compiler_params=pltpu.CompilerParams(
            dimension_semantics=("arbitrary",),
            vmem_limit_bytes=_vmem_limit(blocks, 0, 2 * _nbytes((rows, inner), F32)),
        ),
        name="ssd_prep",
    )(dt_raw, pad(dt_bias), pad(a_log), e3)


HALO = SUBLANES


def _ssd_chunk_kernel(*refs, q_len, n_chunks, nh):
    if n_chunks > 1:
        (z_ref, xbc_ref, halo_ref, cs_ref, dte_ref, ace_ref, ac_ref, s0_ref, cw_ref, cb_ref, dsk_ref, gw_ref,
         y_ref, st_ref, xp_ref) = refs
    else:
        (z_ref, xbc_ref, cs_ref, dte_ref, ace_ref, ac_ref, s0_ref, cw_ref, cb_ref, dsk_ref, gw_ref,
         y_ref, st_ref, xp_ref) = refs
        halo_ref = None
    ci = pl.program_id(1)
    p = SSD_HEAD_DIM
    n = SSD_STATE
    inner = nh * p
    gw_cols = inner // SSD_GROUPS
    hg = nh // SSD_GROUPS
    nprev = SSD_CONV - 1

    @pl.when(ci == 0)
    def _():
        st_ref[...] = s0_ref[...]

    xp_ref[HALO:HALO + q_len, :] = xbc_ref[0]
    if halo_ref is None:
        xp_ref[HALO - nprev:HALO, :] = cs_ref[0]
    else:
        @pl.when(ci == 0)
        def _():
            xp_ref[HALO - nprev:HALO, :] = cs_ref[0]

        @pl.when(ci > 0)
        def _():
            xp_ref[HALO - nprev:HALO, :] = halo_ref[0, HALO - nprev:HALO, :]

    conv = cb_ref[...]
    for j in range(SSD_CONV):
        conv = conv + xp_ref[HALO - nprev + j:HALO - nprev + j + q_len, :] * cw_ref[j:j + 1, :]
    xc = conv * _sigmoid(conv)

    row = lax.broadcasted_iota(jnp.int32, (q_len, q_len), 0)
    col = lax.broadcasted_iota(jnp.int32, (q_len, q_len), 1)
    causal = row >= col
    lane = lax.broadcasted_iota(jnp.int32, (q_len, LANES), 1)
    if q_len >= LANES:
        act = ac_ref[0].T
    ones16 = jnp.where(lax.broadcasted_iota(jnp.int32, (16, n), 0) < 3, 1.0, 0.0).astype(BF16)
    row16 = lax.broadcasted_iota(jnp.int32, (16, gw_cols), 0)

    for g in range(SSD_GROUPS):
        cs0 = g * gw_cols
        xs = xc[:, cs0:cs0 + gw_cols]
        bq = xc[:, inner + g * n:inner + (g + 1) * n]
        cq = xc[:, inner + SSD_GROUPS * n + g * n:inner + SSD_GROUPS * n + (g + 1) * n]
        dte = dte_ref[0, :, cs0:cs0 + gw_cols]
        ace = ace_ref[0, :, cs0:cs0 + gw_cols]
        xdt = xs * dte
        bq16 = bq.astype(BF16)
        cq16 = cq.astype(BF16)
        s_in = st_ref[0, cs0:cs0 + gw_cols, :]

        if q_len >= LANES:
            cb = _dot_nt(cq16, bq16)
            parts = []
            for pi in range(hg // 2):
                ms = []
                for hh in (2 * pi, 2 * pi + 1):
                    hd = g * hg + hh
                    seg = ac_ref[0, :, hd:hd + 1] - act[hd:hd + 1, :]
                    ms.append(cb * jnp.where(causal, jnp.exp(seg), 0.0))
                lhs = jnp.concatenate(ms, axis=1).astype(BF16)
                xpair = xdt[:, pi * LANES:(pi + 1) * LANES]
                rhs = jnp.concatenate([jnp.where(lane < p, xpair, 0.0), jnp.where(lane >= p, xpair, 0.0)],
                                      axis=0).astype(BF16)
                parts.append(_dot(lhs, rhs))
            y = jnp.concatenate(parts, axis=1)
        else:
            rowq = lax.broadcasted_iota(jnp.int32, (q_len, gw_cols), 0)
            y = jnp.zeros((q_len, gw_cols), F32)
            for s in range(q_len):
                cb_s = jnp.sum(cq * bq[s:s + 1, :], axis=1, keepdims=True)
                term = cb_s * jnp.exp(ace - ace[s:s + 1, :]) * xdt[s:s + 1, :]
                y = y + jnp.where(rowq >= s, term, 0.0)

        y = y + _dot_nt(cq16, s_in.astype(BF16)) * jnp.exp(ace)
        ace_last = ace[q_len - 1:q_len, :]
        xw = xdt * jnp.exp(ace_last - ace)
        pad_rows = (-q_len) % 16
        if pad_rows:
            xw = jnp.concatenate([xw, jnp.zeros((pad_rows, gw_cols), F32)], axis=0)
            bqp = jnp.concatenate([bq, jnp.zeros((pad_rows, n), F32)], axis=0).astype(BF16)
        else:
            bqp = bq16
        new_states = _dot_tn(xw.astype(BF16), bqp)
        d_hi, d_mid, d_lo = (t.astype(F32) for t in _split3(jnp.exp(ace_last)))
        dec3 = jnp.where(row16 == 0, d_hi, jnp.where(row16 == 1, d_mid, jnp.where(row16 == 2, d_lo, 0.0)))
        dec_col = _dot_tn(dec3.astype(BF16), ones16)
        st_ref[0, cs0:cs0 + gw_cols, :] = s_in * dec_col + new_states

        y = (y + dsk_ref[:, cs0:cs0 + gw_cols] * xs)
        zz = z_ref[0, :, cs0:cs0 + gw_cols]
        y = y * (zz * _sigmoid(zz))
        ms = jnp.mean(y * y, axis=-1, keepdims=True)
        y = y * lax.rsqrt(ms + RMS_EPS) * gw_ref[:, cs0:cs0 + gw_cols]
        y_ref[0, :, cs0:cs0 + gw_cols] = y.astype(y_ref.dtype)


def ssd_chunk(z, xbc, conv_state, dt_e, acum_e, acum, ssm_state, conv_w, conv_b, d_skip, gnorm_w, *, q_len):
    nb, l, inner = z.shape
    conv_dim = xbc.shape[2]
    nh = inner // SSD_HEAD_DIM
    n = SSD_STATE
    assert l % q_len == 0
    n_chunks = l // q_len
    assert q_len == LANES or (q_len < 16 and n_chunks == 1)
    y_dtype = BF16 if q_len >= 16 else F32
    dsk = jnp.repeat(d_skip.astype(F32), SSD_HEAD_DIM).reshape(1, inner)

    def seq_chunk(b, c):
        return (b, c, 0)

    def seq_only(b, c):
        return (b, 0, 0)

    def const2(b, c):
        return (0, 0)

    in_specs = [pl.BlockSpec((1, q_len, inner), seq_chunk), pl.BlockSpec((1, q_len, conv_dim), seq_chunk)]
    args = [z, xbc]
    if n_chunks > 1:
        per = q_len // HALO
        in_specs.append(pl.BlockSpec((1, HALO, conv_dim), lambda b, c: (b, jnp.maximum(c * per - 1, 0), 0)))
        args.append(xbc)
    in_specs += [
        pl.BlockSpec((1, SSD_CONV - 1, conv_dim), seq_only),
        pl.BlockSpec((1, q_len, inner), seq_chunk),
        pl.BlockSpec((1, q_len, inner), seq_chunk),
        pl.BlockSpec((1, q_len, LANES), seq_chunk),
        pl.BlockSpec((1, inner, n), seq_only),
        pl.BlockSpec((SSD_CONV, conv_dim), const2),
        pl.BlockSpec((1, conv_dim), const2),
        pl.BlockSpec((1, inner), const2),
        pl.BlockSpec((1, inner), const2),
    ]
    args += [conv_state, dt_e, acum_e, acum, ssm_state, conv_w, conv_b.reshape(1, conv_dim), dsk,
             gnorm_w.reshape(1, inner)]
    qp = -(-q_len // SUBLANES) * SUBLANES
    blocks = (4 * _nbytes((q_len, inner), F32) + _nbytes((q_len, conv_dim), F32) + _nbytes((HALO, conv_dim), F32)
              + 2 * _nbytes((inner, n), F32) + _nbytes((q_len, LANES), F32))
    scratch = _nbytes((HALO + qp, conv_dim), F32)
    return pl.pallas_call(
        functools.partial(_ssd_chunk_kernel, q_len=q_len, n_chunks=n_chunks, nh=nh),
        grid=(nb, n_chunks),
        in_specs=in_specs,
        out_specs=[pl.BlockSpec((1, q_len, inner), seq_chunk), pl.BlockSpec((1, inner, n), seq_only)],
        out_shape=[jax.ShapeDtypeStruct((nb, l, inner), y_dtype), jax.ShapeDtypeStruct((nb, inner, n), F32)],
        scratch_shapes=[pltpu.VMEM((HALO + qp, conv_dim), F32)],
        compiler_params=pltpu.CompilerParams(
            dimension_semantics=("arbitrary", "arbitrary"),
            vmem_limit_bytes=_vmem_limit(blocks, scratch, 6 * _nbytes((max(q_len, SUBLANES), conv_dim), F32)),
        ),
        name="ssd_chunk",
    )(*args)


def _row_block(m, target):
    return target if m % target == 0 else m


def _trunk(x, paged, conv_states, ssm_states, p):
    nb, l, dm = x.shape
    m = nb * l
    xf = x.reshape(m, dm)
    bm = _row_block(m, 1024)
    bm_out = _row_block(m, 512)
    ks, vs, convs, ssms = [], [], [], []
    depth = p["depth"]
    for i in range(depth):
        j = i // 2
        if i % 2 == 0:
            w_in, w_out = p["w_in_sb"][j], p["w_out_sb"][j]
            width = w_out.shape[0]
            scale = SB_HEAD_DIM ** -0.5
            proj = functools.partial(norm_matmul, xf, p["norm_sb"][j], w_in, n_cols=width, bm=bm, bn=512)
            k = proj(col_start=width, out_dtype=F32)
            v = proj(col_start=2 * width, out_dtype=F32)
            gate = proj(col_start=3 * width, out_dtype=F32)
            if paged is None:
                q = proj(col_start=0, out_dtype=BF16, out_scale=scale)
                og = jnp.concatenate([
                    sb_prompt_attn(q[b * l:(b + 1) * l], k[b * l:(b + 1) * l], v[b * l:(b + 1) * l],
                                   gate[b * l:(b + 1) * l], p["sb_bias"][j], tq=512, tk=256)
                    for b in range(nb)], axis=0) if nb > 1 else sb_prompt_attn(
                        q, k, v, gate, p["sb_bias"][j], tq=512, tk=256)
            else:
                cache_k, cache_v, page_table, n_phys = paged
                q = proj(col_start=0, out_dtype=F32, out_scale=scale)
                og = sb_paged_attn(q.reshape(nb, l, width), k.reshape(nb, l, width), v.reshape(nb, l, width),
                                   gate.reshape(nb, l, width), p["sb_bias"][j], cache_k, cache_v, page_table,
                                   j * n_phys).reshape(m, width)
            xf = matmul_res(og, w_out, xf, bm=bm_out, bk=512)
            nh = width // SB_HEAD_DIM
            ks.append(k.reshape(nb, l, nh, SB_HEAD_DIM))
            vs.append(v.reshape(nb, l, nh, SB_HEAD_DIM))
        else:
            w_in, w_dt, w_out = p["w_in_ssd"][j], p["w_dt_ssd"][j], p["w_out_ssd"][j]
            inner = w_out.shape[0]
            conv_dim = p["conv_w"].shape[2]
            nh = inner // SSD_HEAD_DIM
            proj = functools.partial(norm_matmul, xf, p["norm_ssd"][j], bm=bm)
            z = proj(w_in, col_start=0, n_cols=inner, out_dtype=F32, bn=512)
            xbc = proj(w_in, col_start=inner, n_cols=conv_dim, out_dtype=F32, bn=512)
            dt_raw = proj(w_dt, col_start=0, n_cols=LANES, out_dtype=F32, bn=LANES)
            q_len = SSD_CHUNK if l % SSD_CHUNK == 0 else l
            dt_e, acum_e, acum = ssd_prep(dt_raw, p["dt_bias"][j], p["a_log"][j], chunk=q_len,
                                          rows=_row_block(m, LANES))
            y, new_ssm = ssd_chunk(
                z.reshape(nb, l, inner), xbc.reshape(nb, l, conv_dim), conv_states[j],
                dt_e.reshape(nb, l, inner), acum_e.reshape(nb, l, inner), acum.reshape(nb, l, LANES),
                ssm_states[j].reshape(nb, inner, SSD_STATE), p["conv_w"][j], p["conv_b"][j], p["d_skip"][j],
                p["gnorm_w"][j], q_len=q_len)
            last = i == depth - 1
            xf = matmul_res(y.reshape(m, inner), w_out, xf, p["norm_f"] if last else None, bm=bm_out, bk=512)
            xbc3 = xbc.reshape(nb, l, conv_dim)
            nprev = SSD_CONV - 1
            if l >= nprev:
                convs.append(xbc3[:, l - nprev:])
            else:
                convs.append(jnp.concatenate([conv_states[j].astype(F32), xbc3], axis=1)[:, l:])
            ssms.append(new_ssm.reshape(nb, nh, SSD_HEAD_DIM, SSD_STATE))
    if depth % 2 == 1:
        raise NotImplementedError("final norm is fused into the last SSD output projection")
    return xf.reshape(nb, l, dm), jnp.stack(ks), jnp.stack(vs), jnp.stack(convs), jnp.stack(ssms)


def kernel(x_prompt, x_sample, cache_k, cache_v, state_conv, state_ssm, page_table, norm_sb, w_in_sb, w_out_sb,
           sb_bias, norm_ssd, w_in_ssd, conv_w, conv_b, dt_bias, a_log, d_skip, gnorm_w, w_out_ssd, norm_f):
    n_sb = w_in_sb.shape[0]
    n_ssd = w_in_ssd.shape[0]
    inner = w_out_ssd.shape[1]
    conv_dim = conv_w.shape[2]
    nh_ssd = dt_bias.shape[1]
    dt_cols = w_in_ssd[:, :, inner + conv_dim:inner + conv_dim + nh_ssd]
    w_dt = jnp.zeros((n_ssd, w_in_ssd.shape[1], LANES), F32).at[:, :, :nh_ssd].set(dt_cols)
    p = dict(
        depth=n_sb + n_ssd,
        norm_sb=norm_sb, w_in_sb=w_in_sb.astype(BF16), w_out_sb=w_out_sb.astype(BF16), sb_bias=sb_bias,
        norm_ssd=norm_ssd, w_in_ssd=w_in_ssd.astype(BF16), w_dt_ssd=w_dt.astype(BF16),
        w_out_ssd=w_out_ssd.astype(BF16), conv_w=conv_w, conv_b=conv_b, dt_bias=dt_bias, a_log=a_log,
        d_skip=d_skip, gnorm_w=gnorm_w, norm_f=norm_f,
    )
    nbp = x_prompt.shape[0]
    zero_conv = jnp.zeros((n_ssd, nbp, SSD_CONV - 1, conv_dim), x_prompt.dtype)
    zero_ssm = jnp.zeros((n_ssd, nbp, nh_ssd, SSD_HEAD_DIM, SSD_STATE), x_prompt.dtype)
    y_p, k_p, v_p, conv_p, ssm_p = _trunk(x_prompt, None, zero_conv, zero_ssm, p)

    n_phys, page = cache_k.shape[1], cache_k.shape[2]
    width = cache_k.shape[3] * cache_k.shape[4]
    paged = (cache_k.reshape(n_sb * n_phys, page, width), cache_v.reshape(n_sb * n_phys, page, width),
             page_table, n_phys)
    y_s, k_s, v_s, conv_s, ssm_s = _trunk(x_sample, paged, state_conv, state_ssm, p)
    return (y_p, y_s, k_p, v_p, conv_p, ssm_p, k_s, v_s, conv_s, ssm_s)
```

```python
import functools

import jax
import jax.numpy as jnp
from jax import lax
from jax.experimental import pallas as pl
from jax.experimental.pallas import tpu as pltpu

F32 = jnp.float32
BF16 = jnp.bfloat16
RMS_EPS = 1e-6

SB_HEAD_DIM = 128
SSD_HEAD_DIM = 64
SSD_GROUPS = 8
SSD_STATE = 128
SSD_CONV = 4
SSD_CHUNK = 128

LANES = 128
SUBLANES = 8
VMEM_PHYSICAL_BYTES = 64 * 1024 * 1024
VMEM_LIMIT_CAP_BYTES = 56 * 1024 * 1024


def _vmem_limit(block_bytes, scratch_bytes=0, temp_bytes=0):
    est = 2 * block_bytes + scratch_bytes + temp_bytes
    return int(min(max(est, 16 * 1024 * 1024), VMEM_LIMIT_CAP_BYTES))


def _nbytes(shape, dtype):
    n = 1
    for s in shape:
        n *= s
    return n * jnp.dtype(dtype).itemsize


def _sigmoid(x):
    return 1.0 / (1.0 + jnp.exp(-x))


def _softplus(x):
    return jnp.maximum(x, 0.0) + jnp.log(1.0 + jnp.exp(-jnp.abs(x)))


def _split2(x):
    hi = x.astype(BF16)
    lo = (x - hi.astype(F32)).astype(BF16)
    return hi, lo


def _split3(x):
    hi = x.astype(BF16)
    r1 = x - hi.astype(F32)
    mid = r1.astype(BF16)
    lo = (r1 - mid.astype(F32)).astype(BF16)
    return hi, mid, lo


def _dot(a, b):
    return jnp.dot(a, b, preferred_element_type=F32)


def _dot_nt(a, b):
    return lax.dot_general(a, b, (((1,), (1,)), ((), ())), preferred_element_type=F32)


def _dot_tn(a, b):
    return lax.dot_general(a, b, (((0,), (0,)), ((), ())), preferred_element_type=F32)


def _norm_matmul_kernel(x_ref, g_ref, w_ref, *rest, out_scale):
    o_refs, xn_ref = rest[:-1], rest[-1]

    @pl.when(pl.program_id(1) == 0)
    def _():
        x = x_ref[...]
        ms = jnp.mean(x * x, axis=-1, keepdims=True)
        xn_ref[...] = (x * lax.rsqrt(ms + RMS_EPS) * g_ref[...]).astype(BF16)

    acc = _dot(xn_ref[...], w_ref[...])
    if out_scale != 1.0:
        acc = acc * out_scale
    for o_ref in o_refs:
        o_ref[...] = acc.astype(o_ref.dtype)


def norm_matmul(x, g, w, *, col_start, n_cols, out_dtypes, out_scale=1.0, bm, bn):
    m, k = x.shape
    assert m % bm == 0 and n_cols % bn == 0 and col_start % bn == 0
    off = col_start // bn
    blocks = _nbytes((bm, k), F32) + _nbytes((k, bn), BF16) + sum(_nbytes((bm, bn), dt) for dt in out_dtypes)
    outs = pl.pallas_call(
        functools.partial(_norm_matmul_kernel, out_scale=out_scale),
        grid=(m // bm, n_cols // bn),
        in_specs=[
            pl.BlockSpec((bm, k), lambda i, j: (i, 0)),
            pl.BlockSpec((1, k), lambda i, j: (0, 0)),
            pl.BlockSpec((k, bn), lambda i, j: (0, j + off)),
        ],
        out_specs=[pl.BlockSpec((bm, bn), lambda i, j: (i, j)) for _ in out_dtypes],
        out_shape=[jax.ShapeDtypeStruct((m, n_cols), dt) for dt in out_dtypes],
        scratch_shapes=[pltpu.VMEM((bm, k), BF16)],
        compiler_params=pltpu.CompilerParams(
            dimension_semantics=("arbitrary", "arbitrary"),
            vmem_limit_bytes=_vmem_limit(blocks, _nbytes((bm, k), BF16), 2 * _nbytes((bm, k), F32)),
        ),
        name="norm_matmul",
    )(x, g.reshape(1, k), w)
    return outs[0] if len(outs) == 1 else tuple(outs)


def _matmul_res_kernel(a_ref, w_ref, r_ref, g_ref, o_ref, *, final_norm):
    kk = pl.program_id(1)

    @pl.when(kk == 0)
    def _():
        o_ref[...] = r_ref[...]

    o_ref[...] += _dot(a_ref[...].astype(BF16), w_ref[...])

    if final_norm:
        @pl.when(kk == pl.num_programs(1) - 1)
        def _():
            x = o_ref[...]
            ms = jnp.mean(x * x, axis=-1, keepdims=True)
            o_ref[...] = x * lax.rsqrt(ms + RMS_EPS) * g_ref[...]


def matmul_res(a, w, res, g=None, *, bm, bk):
    m, k = a.shape
    n = w.shape[1]
    assert m % bm == 0 and k % bk == 0
    final_norm = g is not None
    if g is None:
        g = jnp.ones((n,), F32)
    blocks = _nbytes((bm, bk), a.dtype) + _nbytes((bk, n), BF16) + 2 * _nbytes((bm, n), F32)
    return pl.pallas_call(
        functools.partial(_matmul_res_kernel, final_norm=final_norm),
        grid=(m // bm, k // bk),
        in_specs=[
            pl.BlockSpec((bm, bk), lambda i, j: (i, j)),
            pl.BlockSpec((bk, n), lambda i, j: (j, 0)),
            pl.BlockSpec((bm, n), lambda i, j: (i, 0)),
            pl.BlockSpec((1, n), lambda i, j: (0, 0)),
        ],
        out_specs=pl.BlockSpec((bm, n), lambda i, j: (i, 0)),
        out_shape=jax.ShapeDtypeStruct((m, n), F32),
        compiler_params=pltpu.CompilerParams(
            dimension_semantics=("arbitrary", "arbitrary"),
            vmem_limit_bytes=_vmem_limit(blocks, 0, 2 * _nbytes((bm, n), F32)),
        ),
        name="matmul_res",
    )(a, w, res, g.reshape(1, n))


def _neg_softplus(z):
    return -_softplus(z)


LOG2E = 1.4426950408889634
MASKED_LOG = -1e30


def _sb_prompt_kernel(bias_ref, q_ref, k_ref, v_ref, gate_ref, o_ref, lb_ref, lk_ref, lk0_ref, carry_ref, acc_ref,
                      *, tq, tk):
    h = pl.program_id(0)
    qi = pl.program_id(1)
    bias2 = bias_ref[h] * LOG2E
    nd = tq // tk

    def stage1(kb, slot, masked):
        start = pl.multiple_of(kb * tk, tk)
        z = _dot_nt(q_ref[...], k_ref[pl.ds(start, tk), :]) + bias2
        lb = jnp.minimum(z, 0.0) - jnp.log2(1.0 + jnp.exp2(-jnp.abs(z)))
        lk = lb - z
        if masked:
            t_pos = qi * tq + lax.broadcasted_iota(jnp.int32, (tq, tk), 0)
            s_pos = kb * tk + lax.broadcasted_iota(jnp.int32, (tq, tk), 1)
            lk = jnp.where(s_pos < t_pos, lk, 0.0)
            lb = jnp.where(s_pos < t_pos, lb, MASKED_LOG)
        lb_ref[slot] = lb
        lk_ref[slot] = lk.astype(BF16)
        lk0_ref[slot] = lk[:, 0:1]

    def stage2(kb, slot):
        start = pl.multiple_of(kb * tk, tk)
        r = lax.broadcasted_iota(jnp.int32, (tk, tk), 0)
        c = lax.broadcasted_iota(jnp.int32, (tk, tk), 1)
        tri = jnp.where(r > c, 1.0, 0.0).astype(BF16)
        cs = _dot(lk_ref[slot], tri) + carry_ref[...]
        w = jnp.exp2(lb_ref[slot] + cs)
        acc_ref[...] += _dot(w.astype(BF16), v_ref[pl.ds(start, tk), :])
        carry_ref[...] = cs[:, 0:1] + lk0_ref[slot]

    carry_ref[...] = jnp.zeros_like(carry_ref)
    acc_ref[...] = jnp.zeros_like(acc_ref)
    kb_top = qi * nd + nd - 1
    stage1(kb_top, 0, True)
    for dd in range(1, nd):
        stage1(kb_top - dd, dd % 2, True)
        stage2(kb_top - dd + 1, (dd - 1) % 2)
    pending = (nd - 1) % 2

    def body(i, _):
        kb = qi * nd - 1 - 2 * i
        stage1(kb, 1 - pending, False)
        stage2(kb + 1, pending)
        stage1(kb - 1, pending, False)
        stage2(kb, 1 - pending)
        return 0

    lax.fori_loop(0, (qi * nd) // 2, body, 0)
    stage2(0, pending)
    g = gate_ref[...]
    o_ref[...] = (acc_ref[...] * (g * _sigmoid(g))).astype(o_ref.dtype)


def sb_prompt_attn(q, k, v, gate, bias, *, tq, tk):
    l, width = q.shape
    d = SB_HEAD_DIM
    nh = width // d
    assert l % tq == 0 and tq % (2 * tk) == 0
    blocks = _nbytes((tq, d), BF16) * 2 + 2 * _nbytes((l, d), BF16) + _nbytes((tq, d), F32)
    scratch_shapes = [
        pltpu.VMEM((2, tq, tk), F32),
        pltpu.VMEM((2, tq, tk), BF16),
        pltpu.VMEM((2, tq, 1), F32),
        pltpu.VMEM((tq, 1), F32),
        pltpu.VMEM((tq, d), F32),
    ]
    scratch = 2 * _nbytes((tq, tk), F32) + 2 * _nbytes((tq, tk), BF16) + 4 * _nbytes((tq, LANES), F32)
    return pl.pallas_call(
        functools.partial(_sb_prompt_kernel, tq=tq, tk=tk),
        grid=(nh, l // tq),
        in_specs=[
            pl.BlockSpec(memory_space=pltpu.SMEM),
            pl.BlockSpec((tq, d), lambda h, i: (i, h)),
            pl.BlockSpec((l, d), lambda h, i: (0, h)),
            pl.BlockSpec((l, d), lambda h, i: (0, h)),
            pl.BlockSpec((tq, d), lambda h, i: (i, h)),
        ],
        out_specs=pl.BlockSpec((tq, d), lambda h, i: (i, h)),
        out_shape=jax.ShapeDtypeStruct((l, width), BF16),
        scratch_shapes=scratch_shapes,
        compiler_params=pltpu.CompilerParams(
            dimension_semantics=("arbitrary", "arbitrary"),
            vmem_limit_bytes=_vmem_limit(blocks, scratch, 12 * _nbytes((tq, tk), F32)),
        ),
        name="sb_prompt_attn",
    )(bias, q, k, v, gate)


NEW_ROWS = 16


def _sb_paged_kernel(pt_ref, q_ref, kn_ref, vn_ref, gate_ref, bias_ref, *rest, nq, nh, page, pps):
    kp_refs, vp_refs = rest[:pps], rest[pps:2 * pps]
    o_ref, qrows_ref, kscr_ref, vscr_ref, carry_ref, acc_ref = rest[2 * pps:]
    b = pl.program_id(0)
    p = pl.program_id(1)
    d = SB_HEAD_DIM
    width = nh * d
    ncol = nh * nq

    def suffix_tri(rows):
        r = lax.broadcasted_iota(jnp.int32, (rows, 2 * rows), 0)
        c = lax.broadcasted_iota(jnp.int32, (rows, 2 * rows), 1)
        return jnp.where(jnp.where(c >= rows, c - rows, c) >= r, 1.0, 0.0).astype(BF16)

    def blocks_step(k16, v16, rows, mask):
        n = k16.shape[0] // rows
        tri2 = suffix_tri(rows)
        z = _dot_nt(k16, qrows_ref[...]) + bias_ref[...]
        lk = _neg_softplus(z)
        if mask is not None:
            lk = jnp.where(mask, lk, 0.0)
        hi, lo = _split2(lk)
        carry = carry_ref[...]
        ws = []
        for i in range(n):
            sl = slice(i * rows, (i + 1) * rows)
            cs = _dot(tri2, jnp.concatenate([hi[sl], lo[sl]], axis=0)) + carry
            ws.append(jnp.exp(z[sl] + cs))
            carry = cs[0:1, :]
        w = ws[0] if n == 1 else jnp.concatenate(ws, axis=0)
        if mask is not None:
            w = jnp.where(mask, w, 0.0)
        acc_ref[...] += _dot_tn(w.astype(BF16), v16)
        carry_ref[...] = carry

    def load_pages(refs):
        pages = [jnp.concatenate([ref[pl.ds(hh, page, stride=nh), :] for hh in range(nh)], axis=1).astype(BF16)
                 for ref in refs]
        return pages[0] if pps == 1 else jnp.concatenate(pages, axis=0)

    @pl.when((b == 0) & (p == 0))
    def _():
        kscr_ref[...] = jnp.zeros_like(kscr_ref)
        vscr_ref[...] = jnp.zeros_like(vscr_ref)

    @pl.when(p == 0)
    def _():
        qt = jnp.concatenate([q_ref[0]] * nh, axis=0)
        rh = lax.broadcasted_iota(jnp.int32, (ncol, width), 0) // nq
        ch = lax.broadcasted_iota(jnp.int32, (ncol, width), 1) // d
        qrows_ref[...] = jnp.where(rh == ch, qt, 0.0).astype(BF16)
        carry_ref[...] = jnp.zeros_like(carry_ref)
        acc_ref[...] = jnp.zeros_like(acc_ref)
        kscr_ref[0:nq, :] = kn_ref[0]
        vscr_ref[0:nq, :] = vn_ref[0]
        j = lax.broadcasted_iota(jnp.int32, (NEW_ROWS, ncol), 0)
        i = lax.broadcasted_iota(jnp.int32, (NEW_ROWS, ncol), 1) % nq
        blocks_step(kscr_ref[...].astype(BF16), vscr_ref[...].astype(BF16), NEW_ROWS, j < i)

    blocks_step(load_pages(kp_refs), load_pages(vp_refs), page, None)

    @pl.when(p == pl.num_programs(1) - 1)
    def _():
        acc = acc_ref[...]
        rh = lax.broadcasted_iota(jnp.int32, (ncol, d), 0) // nq
        o = jnp.zeros((ncol, d), F32)
        for hh in range(nh):
            o = o + jnp.where(rh == hh, acc[:, hh * d:(hh + 1) * d], 0.0)
        o = jnp.concatenate([o[hh * nq:(hh + 1) * nq, :] for hh in range(nh)], axis=1)
        g = gate_ref[0]
        o_ref[0] = o * (g * _sigmoid(g))


def sb_paged_attn(q, k_new, v_new, gate, bias, cache_k, cache_v, page_table, page_base, *, page, pps):
    nb, nq, width = q.shape
    d = SB_HEAD_DIM
    nh = width // d
    n_pages = page_table.shape[1]
    assert n_pages % pps == 0 and nq <= NEW_ROWS
    ncol = nh * nq
    bias_cols = jnp.repeat(bias.astype(F32), nq).reshape(1, ncol)
    pt = page_table.reshape(-1).astype(jnp.int32) + page_base

    def tok_map(b, p, pt_ref):
        return (b, 0, 0)

    def page_map(slot):
        return lambda b, p, pt_ref: (pt_ref[b * n_pages + (n_pages - 1 - p * pps - slot)], 0)

    page_specs = [pl.BlockSpec((page * nh, d), page_map(s)) for s in range(pps)]
    blocks = 5 * _nbytes((nq, width), F32) + 2 * pps * _nbytes((page, width), F32)
    scratch = (_nbytes((ncol, width), BF16) + 2 * _nbytes((NEW_ROWS, width), F32) + _nbytes((ncol, width), F32))
    grid_spec = pltpu.PrefetchScalarGridSpec(
        num_scalar_prefetch=1,
        grid=(nb, n_pages // pps),
        in_specs=[
            pl.BlockSpec((1, nq, width), tok_map),
            pl.BlockSpec((1, nq, width), tok_map),
            pl.BlockSpec((1, nq, width), tok_map),
            pl.BlockSpec((1, nq, width), tok_map),
            pl.BlockSpec((1, ncol), lambda b, p, pt_ref: (0, 0)),
        ] + page_specs + page_specs,
        out_specs=pl.BlockSpec((1, nq, width), tok_map),
        scratch_shapes=[
            pltpu.VMEM((ncol, width), BF16),
            pltpu.VMEM((NEW_ROWS, width), F32),
            pltpu.VMEM((NEW_ROWS, width), F32),
            pltpu.VMEM((1, ncol), F32),
            pltpu.VMEM((ncol, width), F32),
        ],
    )
    return pl.pallas_call(
        functools.partial(_sb_paged_kernel, nq=nq, nh=nh, page=page, pps=pps),
        grid_spec=grid_spec,
        out_shape=jax.ShapeDtypeStruct((nb, nq, width), F32),
        compiler_params=pltpu.CompilerParams(
            dimension_semantics=("arbitrary", "arbitrary"),
            vmem_limit_bytes=_vmem_limit(blocks, scratch, 3 * pps * _nbytes((page, width), F32)),
        ),
        name="sb_paged_attn",
    )(pt, q, k_new, v_new, gate, bias_cols, *([cache_k] * pps), *([cache_v] * pps))


def _ssd_prep_kernel(dtraw_ref, dtb_ref, alog_ref, e3_ref, dte_ref, ace_ref, ac_ref, *, chunk):
    rows = dtraw_ref.shape[0]
    dt = _softplus(dtraw_ref[...] + dtb_ref[...])
    a = -jnp.exp(alog_ref[...])
    da = dt * a
    q = lax.broadcasted_iota(jnp.int32, (rows, 3 * rows), 0)
    s = lax.broadcasted_iota(jnp.int32, (rows, 3 * rows), 1) % rows
    tri3 = jnp.where((s <= q) & (s // chunk == q // chunk), 1.0, 0.0).astype(BF16)
    acum = _dot(tri3, jnp.concatenate(_split3(da), axis=0))
    ac_ref[...] = acum
    dte_ref[...] = _dot(jnp.concatenate(_split3(dt), axis=1), e3_ref[...])
    ace_ref[...] = _dot(jnp.concatenate(_split3(acum), axis=1), e3_ref[...])


def ssd_prep(dt_raw, dt_bias, a_log, *, chunk, rows):
    m = dt_raw.shape[0]
    nh = dt_bias.shape[0]
    inner = nh * SSD_HEAD_DIM
    assert m % rows == 0 and rows % chunk == 0 and nh <= LANES

    def pad(v):
        return jnp.zeros((1, LANES), F32).at[0, :nh].set(v.astype(F32))

    lane = jnp.arange(3 * LANES) % LANES
    e3 = (lane[:, None] == (jnp.arange(inner) // SSD_HEAD_DIM)[None, :]).astype(BF16)
    blocks = 2 * _nbytes((rows, LANES), F32) + _nbytes((3 * LANES, inner), BF16) + 2 * _nbytes((rows, inner), F32)
    return pl.pallas_call(
        functools.partial(_ssd_prep_kernel, chunk=chunk),
        grid=(m // rows,),
        in_specs=[
            pl.BlockSpec((rows, LANES), lambda i: (i, 0)),
            pl.BlockSpec((1, LANES), lambda i: (0, 0)),
            pl.BlockSpec((1, LANES), lambda i: (0, 0)),
            pl.BlockSpec((3 * LANES, inner), lambda i: (0, 0)),
        ],
        out_specs=[
            pl.BlockSpec((rows, inner), lambda i: (i, 0)),
            pl.BlockSpec((rows, inner), lambda i: (i, 0)),
            pl.BlockSpec((rows, LANES), lambda i: (i, 0)),
        ],
        out_shape=[
            jax.ShapeDtypeStruct((m, inner), F32),
            jax.ShapeDtypeStruct((m, inner), F32),
            jax.ShapeDtypeStruct((m, LANES), F32),
        ],
        compiler_params=pltpu.CompilerParams(
            dimension_semantics=("arbitrary",),
            vmem_limit_bytes=_vmem_limit(blocks, 0, 2 * _nbytes((rows, inner), F32)),
        ),
        name="ssd_prep",
    )(dt_raw, pad(dt_bias), pad(a_log), e3)


HALO = SUBLANES


def _ssd_chunk_kernel(*refs, q_len, n_chunks, nh):
    if n_chunks > 1:
        (z_ref, xbc_ref, halo_ref, cs_ref, dte_ref, ace_ref, ac_ref, s0_ref, cw_ref, cb_ref, dsk_ref, gw_ref,
         y_ref, st_ref, xp_ref) = refs
    else:
        (z_ref, xbc_ref, cs_ref, dte_ref, ace_ref, ac_ref, s0_ref, cw_ref, cb_ref, dsk_ref, gw_ref,
         y_ref, st_ref, xp_ref) = refs
        halo_ref = None
    ci = pl.program_id(1)
    p = SSD_HEAD_DIM
    n = SSD_STATE
    inner = nh * p
    gw_cols = inner // SSD_GROUPS
    hg = nh // SSD_GROUPS
    nprev = SSD_CONV - 1

    @pl.when(ci == 0)
    def _():
        st_ref[...] = s0_ref[...]

    xp_ref[HALO:HALO + q_len, :] = xbc_ref[0]
    if halo_ref is None:
        xp_ref[HALO - nprev:HALO, :] = cs_ref[0]
    else:
        @pl.when(ci == 0)
        def _():
            xp_ref[HALO - nprev:HALO, :] = cs_ref[0]

        @pl.when(ci > 0)
        def _():
            xp_ref[HALO - nprev:HALO, :] = halo_ref[0, HALO - nprev:HALO, :]

    conv = cb_ref[...]
    for j in range(SSD_CONV):
        conv = conv + xp_ref[HALO - nprev + j:HALO - nprev + j + q_len, :] * cw_ref[j:j + 1, :]
    xc = conv * _sigmoid(conv)

    row = lax.broadcasted_iota(jnp.int32, (q_len, q_len), 0)
    col = lax.broadcasted_iota(jnp.int32, (q_len, q_len), 1)
    causal = row >= col
    lane = lax.broadcasted_iota(jnp.int32, (q_len, LANES), 1)
    if q_len >= LANES:
        act = ac_ref[0].T
    ones16 = jnp.where(lax.broadcasted_iota(jnp.int32, (16, n), 0) < 3, 1.0, 0.0).astype(BF16)
    row16 = lax.broadcasted_iota(jnp.int32, (16, gw_cols), 0)

    for g in range(SSD_GROUPS):
        cs0 = g * gw_cols
        xs = xc[:, cs0:cs0 + gw_cols]
        bq = xc[:, inner + g * n:inner + (g + 1) * n]
        cq = xc[:, inner + SSD_GROUPS * n + g * n:inner + SSD_GROUPS * n + (g + 1) * n]
        dte = dte_ref[0, :, cs0:cs0 + gw_cols]
        ace = ace_ref[0, :, cs0:cs0 + gw_cols]
        xdt = xs * dte
        bq16 = bq.astype(BF16)
        cq16 = cq.astype(BF16)
        s_in = st_ref[0, cs0:cs0 + gw_cols, :]

        if q_len >= LANES:
            cb = _dot_nt(cq16, bq16)
            parts = []
            for pi in range(hg // 2):
                ms = []
                for hh in (2 * pi, 2 * pi + 1):
                    hd = g * hg + hh
                    seg = ac_ref[0, :, hd:hd + 1] - act[hd:hd + 1, :]
                    ms.append(cb * jnp.where(causal, jnp.exp(seg), 0.0))
                lhs = jnp.concatenate(ms, axis=1).astype(BF16)
                xpair = xdt[:, pi * LANES:(pi + 1) * LANES]
                rhs = jnp.concatenate([jnp.where(lane < p, xpair, 0.0), jnp.where(lane >= p, xpair, 0.0)],
                                      axis=0).astype(BF16)
                parts.append(_dot(lhs, rhs))
            y = jnp.concatenate(parts, axis=1)
        else:
            rowq = lax.broadcasted_iota(jnp.int32, (q_len, gw_cols), 0)
            y = jnp.zeros((q_len, gw_cols), F32)
            for s in range(q_len):
                cb_s = jnp.sum(cq * bq[s:s + 1, :], axis=1, keepdims=True)
                term = cb_s * jnp.exp(ace - ace[s:s + 1, :]) * xdt[s:s + 1, :]
                y = y + jnp.where(rowq >= s, term, 0.0)

        y = y + _dot_nt(cq16, s_in.astype(BF16)) * jnp.exp(ace)
        ace_last = ace[q_len - 1:q_len, :]
        xw = xdt * jnp.exp(ace_last - ace)
        pad_rows = (-q_len) % 16
        if pad_rows:
            xw = jnp.concatenate([xw, jnp.zeros((pad_rows, gw_cols), F32)], axis=0)
            bqp = jnp.concatenate([bq, jnp.zeros((pad_rows, n), F32)], axis=0).astype(BF16)
        else:
            bqp = bq16
        new_states = _dot_tn(xw.astype(BF16), bqp)
        d_hi, d_mid, d_lo = (t.astype(F32) for t in _split3(jnp.exp(ace_last)))
        dec3 = jnp.where(row16 == 0, d_hi, jnp.where(row16 == 1, d_mid, jnp.where(row16 == 2, d_lo, 0.0)))
        dec_col = _dot_tn(dec3.astype(BF16), ones16)
        st_ref[0, cs0:cs0 + gw_cols, :] = s_in * dec_col + new_states

        y = (y + dsk_ref[:, cs0:cs0 + gw_cols] * xs)
        zz = z_ref[0, :, cs0:cs0 + gw_cols]
        y = y * (zz * _sigmoid(zz))
        ms = jnp.mean(y * y, axis=-1, keepdims=True)
        y = y * lax.rsqrt(ms + RMS_EPS) * gw_ref[:, cs0:cs0 + gw_cols]
        y_ref[0, :, cs0:cs0 + gw_cols] = y.astype(y_ref.dtype)


def ssd_chunk(z, xbc, conv_state, dt_e, acum_e, acum, ssm_state, conv_w, conv_b, d_skip, gnorm_w, *, q_len):
    nb, l, inner = z.shape
    conv_dim = xbc.shape[2]
    nh = inner // SSD_HEAD_DIM
    n = SSD_STATE
    assert l % q_len == 0
    n_chunks = l // q_len
    assert q_len == LANES or (q_len < 16 and n_chunks == 1)
    y_dtype = BF16 if q_len >= 16 else F32
    dsk = jnp.repeat(d_skip.astype(F32), SSD_HEAD_DIM).reshape(1, inner)

    def seq_chunk(b, c):
        return (b, c, 0)

    def seq_only(b, c):
        return (b, 0, 0)

    def const2(b, c):
        return (0, 0)

    in_specs = [pl.BlockSpec((1, q_len, inner), seq_chunk), pl.BlockSpec((1, q_len, conv_dim), seq_chunk)]
    args = [z, xbc]
    if n_chunks > 1:
        per = q_len // HALO
        in_specs.append(pl.BlockSpec((1, HALO, conv_dim), lambda b, c: (b, jnp.maximum(c * per - 1, 0), 0)))
        args.append(xbc)
    in_specs += [
        pl.BlockSpec((1, SSD_CONV - 1, conv_dim), seq_only),
        pl.BlockSpec((1, q_len, inner), seq_chunk),
        pl.BlockSpec((1, q_len, inner), seq_chunk),
        pl.BlockSpec((1, q_len, LANES), seq_chunk),
        pl.BlockSpec((1, inner, n), seq_only),
        pl.BlockSpec((SSD_CONV, conv_dim), const2),
        pl.BlockSpec((1, conv_dim), const2),
        pl.BlockSpec((1, inner), const2),
        pl.BlockSpec((1, inner), const2),
    ]
    args += [conv_state, dt_e, acum_e, acum, ssm_state, conv_w, conv_b.reshape(1, conv_dim), dsk,
             gnorm_w.reshape(1, inner)]
    qp = -(-q_len // SUBLANES) * SUBLANES
    blocks = (4 * _nbytes((q_len, inner), F32) + _nbytes((q_len, conv_dim), F32) + _nbytes((HALO, conv_dim), F32)
              + 2 * _nbytes((inner, n), F32) + _nbytes((q_len, LANES), F32))
    scratch = _nbytes((HALO + qp, conv_dim), F32)
    return pl.pallas_call(
        functools.partial(_ssd_chunk_kernel, q_len=q_len, n_chunks=n_chunks, nh=nh),
        grid=(nb, n_chunks),
        in_specs=in_specs,
        out_specs=[pl.BlockSpec((1, q_len, inner), seq_chunk), pl.BlockSpec((1, inner, n), seq_only)],
        out_shape=[jax.ShapeDtypeStruct((nb, l, inner), y_dtype), jax.ShapeDtypeStruct((nb, inner, n), F32)],
        scratch_shapes=[pltpu.VMEM((HALO + qp, conv_dim), F32)],
        compiler_params=pltpu.CompilerParams(
            dimension_semantics=("arbitrary", "arbitrary"),
            vmem_limit_bytes=_vmem_limit(blocks, scratch, 6 * _nbytes((max(q_len, SUBLANES), conv_dim), F32)),
        ),
        name="ssd_chunk",
    )(*args)


def _row_block(m, target):
    return target if m % target == 0 else m


def _trunk(x, paged, conv_states, ssm_states, p):
    nb, l, dm = x.shape
    m = nb * l
    xf = x.reshape(m, dm)
    bm = _row_block(m, 1024)
    bm_out = _row_block(m, 512)
    ks, vs, convs, ssms = [], [], [], []
    depth = p["depth"]
    for i in range(depth):
        j = i // 2
        if i % 2 == 0:
            w_in, w_out = p["w_in_sb"][j], p["w_out_sb"][j]
            width = w_out.shape[0]
            scale = SB_HEAD_DIM ** -0.5
            proj = functools.partial(norm_matmul, xf, p["norm_sb"][j], w_in, n_cols=width, bm=bm, bn=512)
            gate = proj(col_start=3 * width, out_dtypes=(F32,))
            if paged is None:
                k, k16 = proj(col_start=width, out_dtypes=(F32, BF16))
                v, v16 = proj(col_start=2 * width, out_dtypes=(F32, BF16))
                q = proj(col_start=0, out_dtypes=(BF16,), out_scale=scale * LOG2E)
                og = jnp.concatenate([
                    sb_prompt_attn(q[b * l:(b + 1) * l], k16[b * l:(b + 1) * l], v16[b * l:(b + 1) * l],
                                   gate[b * l:(b + 1) * l], p["sb_bias"][j], tq=512, tk=256)
                    for b in range(nb)], axis=0)
            else:
                cache_k, cache_v, page_table, n_phys, page = paged
                k = proj(col_start=width, out_dtypes=(F32,))
                v = proj(col_start=2 * width, out_dtypes=(F32,))
                q = proj(col_start=0, out_dtypes=(F32,), out_scale=scale)
                og = sb_paged_attn(q.reshape(nb, l, width), k.reshape(nb, l, width), v.reshape(nb, l, width),
                                   gate.reshape(nb, l, width), p["sb_bias"][j], cache_k, cache_v, page_table,
                                   j * n_phys, page=page,
                                   pps=max(c for c in (4, 2, 1) if page_table.shape[1] % c == 0)).reshape(m, width)
            xf = matmul_res(og, w_out, xf, bm=bm_out, bk=512)
            nh = width // SB_HEAD_DIM
            ks.append(k.reshape(nb, l, nh, SB_HEAD_DIM))
            vs.append(v.reshape(nb, l, nh, SB_HEAD_DIM))
        else:
            w_in, w_dt, w_out = p["w_in_ssd"][j], p["w_dt_ssd"][j], p["w_out_ssd"][j]
            inner = w_out.shape[0]
            conv_dim = p["conv_w"].shape[2]
            nh = inner // SSD_HEAD_DIM
            proj = functools.partial(norm_matmul, xf, p["norm_ssd"][j], bm=bm)
            z = proj(w_in, col_start=0, n_cols=inner, out_dtypes=(F32,), bn=512)
            xbc = proj(w_in, col_start=inner, n_cols=conv_dim, out_dtypes=(F32,), bn=512)
            dt_raw = proj(w_dt, col_start=0, n_cols=LANES, out_dtypes=(F32,), bn=LANES)
            q_len = SSD_CHUNK if l % SSD_CHUNK == 0 else l
            dt_e, acum_e, acum = ssd_prep(dt_raw, p["dt_bias"][j], p["a_log"][j], chunk=q_len,
                                          rows=_row_block(m, LANES))
            y, new_ssm = ssd_chunk(
                z.reshape(nb, l, inner), xbc.reshape(nb, l, conv_dim), conv_states[j],
                dt_e.reshape(nb, l, inner), acum_e.reshape(nb, l, inner), acum.reshape(nb, l, LANES),
                ssm_states[j].reshape(nb, inner, SSD_STATE), p["conv_w"][j], p["conv_b"][j], p["d_skip"][j],
                p["gnorm_w"][j], q_len=q_len)
            last = i == depth - 1
            xf = matmul_res(y.reshape(m, inner), w_out, xf, p["norm_f"] if last else None, bm=bm_out, bk=512)
            xbc3 = xbc.reshape(nb, l, conv_dim)
            nprev = SSD_CONV - 1
            if l >= nprev:
                convs.append(xbc3[:, l - nprev:])
            else:
                convs.append(jnp.concatenate([conv_states[j].astype(F32), xbc3], axis=1)[:, l:])
            ssms.append(new_ssm.reshape(nb, nh, SSD_HEAD_DIM, SSD_STATE))
    if depth % 2 == 1:
        raise NotImplementedError("final norm is fused into the last SSD output projection")
    return xf.reshape(nb, l, dm), jnp.stack(ks), jnp.stack(vs), jnp.stack(convs), jnp.stack(ssms)


def kernel(x_prompt, x_sample, cache_k, cache_v, state_conv, state_ssm, page_table, norm_sb, w_in_sb, w_out_sb,
           sb_bias, norm_ssd, w_in_ssd, conv_w, conv_b, dt_bias, a_log, d_skip, gnorm_w, w_out_ssd, norm_f):
    n_sb = w_in_sb.shape[0]
    n_ssd = w_in_ssd.shape[0]
    inner = w_out_ssd.shape[1]
    conv_dim = conv_w.shape[2]
    nh_ssd = dt_bias.shape[1]
    dt_cols = w_in_ssd[:, :, inner + conv_dim:inner + conv_dim + nh_ssd]
    w_dt = jnp.zeros((n_ssd, w_in_ssd.shape[1], LANES), F32).at[:, :, :nh_ssd].set(dt_cols)
    p = dict(
        depth=n_sb + n_ssd,
        norm_sb=norm_sb, w_in_sb=w_in_sb.astype(BF16), w_out_sb=w_out_sb.astype(BF16), sb_bias=sb_bias,
        norm_ssd=norm_ssd, w_in_ssd=w_in_ssd[:, :, :inner + conv_dim].astype(BF16), w_dt_ssd=w_dt.astype(BF16),
        w_out_ssd=w_out_ssd.astype(BF16), conv_w=conv_w, conv_b=conv_b, dt_bias=dt_bias, a_log=a_log,
        d_skip=d_skip, gnorm_w=gnorm_w, norm_f=norm_f,
    )
    nbp = x_prompt.shape[0]
    zero_conv = jnp.zeros((n_ssd, nbp, SSD_CONV - 1, conv_dim), x_prompt.dtype)
    zero_ssm = jnp.zeros((n_ssd, nbp, nh_ssd, SSD_HEAD_DIM, SSD_STATE), x_prompt.dtype)
    y_p, k_p, v_p, conv_p, ssm_p = _trunk(x_prompt, None, zero_conv, zero_ssm, p)

    n_phys, page = cache_k.shape[1], cache_k.shape[2]
    paged = (cache_k.reshape(-1, cache_k.shape[4]), cache_v.reshape(-1, cache_v.shape[4]), page_table, n_phys, page)
    y_s, k_s, v_s, conv_s, ssm_s = _trunk(x_sample, paged, state_conv, state_ssm, p)
    return (y_p, y_s, k_p, v_p, conv_p, ssm_p, k_s, v_s, conv_s, ssm_s)
```

```python
import functools

import jax
import jax.numpy as jnp
from jax import lax
from jax.experimental import pallas as pl
from jax.experimental.pallas import tpu as pltpu

F32 = jnp.float32
BF16 = jnp.bfloat16
RMS_EPS = 1e-6

SB_HEAD_DIM = 128
SSD_HEAD_DIM = 64
SSD_GROUPS = 8
SSD_STATE = 128
SSD_CONV = 4
SSD_CHUNK = 128

LANES = 128
SUBLANES = 8
VMEM_PHYSICAL_BYTES = 64 * 1024 * 1024
VMEM_LIMIT_CAP_BYTES = 56 * 1024 * 1024


def _vmem_limit(block_bytes, scratch_bytes=0, temp_bytes=0):
    est = 2 * block_bytes + scratch_bytes + temp_bytes
    return int(min(max(est, 16 * 1024 * 1024), VMEM_LIMIT_CAP_BYTES))


def _nbytes(shape, dtype):
    n = 1
    for s in shape:
        n *= s
    return n * jnp.dtype(dtype).itemsize


def _sigmoid(x):
    return 1.0 / (1.0 + jnp.exp(-x))


def _softplus(x):
    return jnp.maximum(x, 0.0) + jnp.log(1.0 + jnp.exp(-jnp.abs(x)))


def _split2(x):
    hi = x.astype(BF16)
    lo = (x - hi.astype(F32)).astype(BF16)
    return hi, lo


def _split3(x):
    hi = x.astype(BF16)
    r1 = x - hi.astype(F32)
    mid = r1.astype(BF16)
    lo = (r1 - mid.astype(F32)).astype(BF16)
    return hi, mid, lo


def _dot(a, b):
    return jnp.dot(a, b, preferred_element_type=F32)


def _dot_nt(a, b):
    return lax.dot_general(a, b, (((1,), (1,)), ((), ())), preferred_element_type=F32)


def _dot_tn(a, b):
    return lax.dot_general(a, b, (((0,), (0,)), ((), ())), preferred_element_type=F32)


def _norm_matmul_kernel(x_ref, g_ref, w_ref, *rest, out_scale):
    o_refs, xn_ref = rest[:-1], rest[-1]

    @pl.when(pl.program_id(1) == 0)
    def _():
        x = x_ref[...]
        ms = jnp.mean(x * x, axis=-1, keepdims=True)
        xn_ref[...] = (x * lax.rsqrt(ms + RMS_EPS) * g_ref[...]).astype(BF16)

    acc = _dot(xn_ref[...], w_ref[...])
    if out_scale != 1.0:
        acc = acc * out_scale
    for o_ref in o_refs:
        o_ref[...] = acc.astype(o_ref.dtype)


def norm_matmul(x, g, w, *, col_start, n_cols, out_dtypes, out_scale=1.0, bm, bn):
    m, k = x.shape
    assert m % bm == 0 and n_cols % bn == 0 and col_start % bn == 0
    off = col_start // bn
    blocks = _nbytes((bm, k), F32) + _nbytes((k, bn), BF16) + sum(_nbytes((bm, bn), dt) for dt in out_dtypes)
    outs = pl.pallas_call(
        functools.partial(_norm_matmul_kernel, out_scale=out_scale),
        grid=(m // bm, n_cols // bn),
        in_specs=[
            pl.BlockSpec((bm, k), lambda i, j: (i, 0)),
            pl.BlockSpec((1, k), lambda i, j: (0, 0)),
            pl.BlockSpec((k, bn), lambda i, j: (0, j + off)),
        ],
        out_specs=[pl.BlockSpec((bm, bn), lambda i, j: (i, j)) for _ in out_dtypes],
        out_shape=[jax.ShapeDtypeStruct((m, n_cols), dt) for dt in out_dtypes],
        scratch_shapes=[pltpu.VMEM((bm, k), BF16)],
        compiler_params=pltpu.CompilerParams(
            dimension_semantics=("arbitrary", "arbitrary"),
            vmem_limit_bytes=_vmem_limit(blocks, _nbytes((bm, k), BF16), 2 * _nbytes((bm, k), F32)),
        ),
        name="norm_matmul",
    )(x, g.reshape(1, k), w)
    return outs[0] if len(outs) == 1 else tuple(outs)


def _matmul_res_kernel(a_ref, w_ref, r_ref, g_ref, o_ref, *, final_norm):
    kk = pl.program_id(1)

    @pl.when(kk == 0)
    def _():
        o_ref[...] = r_ref[...]

    o_ref[...] += _dot(a_ref[...].astype(BF16), w_ref[...])

    if final_norm:
        @pl.when(kk == pl.num_programs(1) - 1)
        def _():
            x = o_ref[...]
            ms = jnp.mean(x * x, axis=-1, keepdims=True)
            o_ref[...] = x * lax.rsqrt(ms + RMS_EPS) * g_ref[...]


def matmul_res(a, w, res, g=None, *, bm, bk):
    m, k = a.shape
    n = w.shape[1]
    assert m % bm == 0 and k % bk == 0
    final_norm = g is not None
    if g is None:
        g = jnp.ones((n,), F32)
    blocks = _nbytes((bm, bk), a.dtype) + _nbytes((bk, n), BF16) + 2 * _nbytes((bm, n), F32)
    return pl.pallas_call(
        functools.partial(_matmul_res_kernel, final_norm=final_norm),
        grid=(m // bm, k // bk),
        in_specs=[
            pl.BlockSpec((bm, bk), lambda i, j: (i, j)),
            pl.BlockSpec((bk, n), lambda i, j: (j, 0)),
            pl.BlockSpec((bm, n), lambda i, j: (i, 0)),
            pl.BlockSpec((1, n), lambda i, j: (0, 0)),
        ],
        out_specs=pl.BlockSpec((bm, n), lambda i, j: (i, 0)),
        out_shape=jax.ShapeDtypeStruct((m, n), F32),
        compiler_params=pltpu.CompilerParams(
            dimension_semantics=("arbitrary", "arbitrary"),
            vmem_limit_bytes=_vmem_limit(blocks, 0, 2 * _nbytes((bm, n), F32)),
        ),
        name="matmul_res",
    )(a, w, res, g.reshape(1, n))


def _neg_softplus(z):
    return -_softplus(z)


LOG2E = 1.4426950408889634
MASKED_LOG = -1e30


def _sb_prompt_kernel(bias_ref, q_ref, k_ref, v_ref, gate_ref, o_ref, lb_ref, lk_ref, lk0_ref, carry_ref, acc_ref,
                      *, tq, tk):
    h = pl.program_id(0)
    qi = pl.program_id(1)
    bias2 = bias_ref[h] * LOG2E
    nd = tq // tk

    def stage1(kb, slot, masked):
        start = pl.multiple_of(kb * tk, tk)
        z = _dot_nt(q_ref[...], k_ref[pl.ds(start, tk), :]) + bias2
        lb = jnp.minimum(z, 0.0) - jnp.log2(1.0 + jnp.exp2(-jnp.abs(z)))
        lk = lb - z
        if masked:
            t_pos = qi * tq + lax.broadcasted_iota(jnp.int32, (tq, tk), 0)
            s_pos = kb * tk + lax.broadcasted_iota(jnp.int32, (tq, tk), 1)
            lk = jnp.where(s_pos < t_pos, lk, 0.0)
            lb = jnp.where(s_pos < t_pos, lb, MASKED_LOG)
        lb_ref[slot] = lb
        lk_ref[slot] = lk.astype(BF16)
        lk0_ref[slot] = lk[:, 0:1]

    def stage2(kb, slot):
        start = pl.multiple_of(kb * tk, tk)
        r = lax.broadcasted_iota(jnp.int32, (tk, tk), 0)
        c = lax.broadcasted_iota(jnp.int32, (tk, tk), 1)
        tri = jnp.where(r > c, 1.0, 0.0).astype(BF16)
        cs = _dot(lk_ref[slot], tri) + carry_ref[...]
        w = jnp.exp2(lb_ref[slot] + cs)
        acc_ref[...] += _dot(w.astype(BF16), v_ref[pl.ds(start, tk), :])
        carry_ref[...] = cs[:, 0:1] + lk0_ref[slot]

    carry_ref[...] = jnp.zeros_like(carry_ref)
    acc_ref[...] = jnp.zeros_like(acc_ref)
    kb_top = qi * nd + nd - 1
    stage1(kb_top, 0, True)
    for dd in range(1, nd):
        stage1(kb_top - dd, dd % 2, True)
        stage2(kb_top - dd + 1, (dd - 1) % 2)
    pending = (nd - 1) % 2

    def body(i, _):
        kb = qi * nd - 1 - 2 * i
        stage1(kb, 1 - pending, False)
        stage2(kb + 1, pending)
        stage1(kb - 1, pending, False)
        stage2(kb, 1 - pending)
        return 0

    lax.fori_loop(0, (qi * nd) // 2, body, 0)
    stage2(0, pending)
    g = gate_ref[...]
    o_ref[...] = (acc_ref[...] * (g * _sigmoid(g))).astype(o_ref.dtype)


def sb_prompt_attn(q, k, v, gate, bias, *, tq, tk):
    l, width = q.shape
    d = SB_HEAD_DIM
    nh = width // d
    assert l % tq == 0 and tq % (2 * tk) == 0
    blocks = _nbytes((tq, d), BF16) * 2 + 2 * _nbytes((l, d), BF16) + _nbytes((tq, d), F32)
    scratch_shapes = [
        pltpu.VMEM((2, tq, tk), F32),
        pltpu.VMEM((2, tq, tk), BF16),
        pltpu.VMEM((2, tq, 1), F32),
        pltpu.VMEM((tq, 1), F32),
        pltpu.VMEM((tq, d), F32),
    ]
    scratch = 2 * _nbytes((tq, tk), F32) + 2 * _nbytes((tq, tk), BF16) + 4 * _nbytes((tq, LANES), F32)
    return pl.pallas_call(
        functools.partial(_sb_prompt_kernel, tq=tq, tk=tk),
        grid=(nh, l // tq),
        in_specs=[
            pl.BlockSpec(memory_space=pltpu.SMEM),
            pl.BlockSpec((tq, d), lambda h, i: (i, h)),
            pl.BlockSpec((l, d), lambda h, i: (0, h)),
            pl.BlockSpec((l, d), lambda h, i: (0, h)),
            pl.BlockSpec((tq, d), lambda h, i: (i, h)),
        ],
        out_specs=pl.BlockSpec((tq, d), lambda h, i: (i, h)),
        out_shape=jax.ShapeDtypeStruct((l, width), BF16),
        scratch_shapes=scratch_shapes,
        compiler_params=pltpu.CompilerParams(
            dimension_semantics=("arbitrary", "arbitrary"),
            vmem_limit_bytes=_vmem_limit(blocks, scratch, 12 * _nbytes((tq, tk), F32)),
        ),
        name="sb_prompt_attn",
    )(bias, q, k, v, gate)


NEW_ROWS = 16


def _sb_paged_kernel(pt_ref, q_ref, kn_ref, vn_ref, gate_ref, bias_ref, *rest, nq, nh, page, pps):
    kp_refs, vp_refs = rest[:pps], rest[pps:2 * pps]
    o_ref, qrows_ref, kscr_ref, vscr_ref, carry_ref, acc_ref = rest[2 * pps:]
    b = pl.program_id(0)
    p = pl.program_id(1)
    d = SB_HEAD_DIM
    width = nh * d
    ncol = nh * nq

    def suffix_tri(rows):
        r = lax.broadcasted_iota(jnp.int32, (rows, 2 * rows), 0)
        c = lax.broadcasted_iota(jnp.int32, (rows, 2 * rows), 1)
        return jnp.where(jnp.where(c >= rows, c - rows, c) >= r, 1.0, 0.0).astype(BF16)

    def blocks_step(k16, v16, rows, mask):
        n = k16.shape[0] // rows
        tri2 = suffix_tri(rows)
        z = _dot_nt(k16, qrows_ref[...]) + bias_ref[...]
        lk = _neg_softplus(z)
        if mask is not None:
            lk = jnp.where(mask, lk, 0.0)
        hi, lo = _split2(lk)
        carry = carry_ref[...]
        ws = []
        for i in range(n):
            sl = slice(i * rows, (i + 1) * rows)
            cs = _dot(tri2, jnp.concatenate([hi[sl], lo[sl]], axis=0)) + carry
            ws.append(jnp.exp(z[sl] + cs))
            carry = cs[0:1, :]
        w = ws[0] if n == 1 else jnp.concatenate(ws, axis=0)
        if mask is not None:
            w = jnp.where(mask, w, 0.0)
        acc_ref[...] += _dot_tn(w.astype(BF16), v16)
        carry_ref[...] = carry

    def load_pages(refs):
        pages = [jnp.concatenate([ref[pl.ds(hh, page, stride=nh), :] for hh in range(nh)], axis=1).astype(BF16)
                 for ref in refs]
        return pages[0] if pps == 1 else jnp.concatenate(pages, axis=0)

    @pl.when((b == 0) & (p == 0))
    def _():
        kscr_ref[...] = jnp.zeros_like(kscr_ref)
        vscr_ref[...] = jnp.zeros_like(vscr_ref)

    @pl.when(p == 0)
    def _():
        qt = jnp.concatenate([q_ref[0]] * nh, axis=0)
        rh = lax.broadcasted_iota(jnp.int32, (ncol, width), 0) // nq
        ch = lax.broadcasted_iota(jnp.int32, (ncol, width), 1) // d
        qrows_ref[...] = jnp.where(rh == ch, qt, 0.0).astype(BF16)
        carry_ref[...] = jnp.zeros_like(carry_ref)
        acc_ref[...] = jnp.zeros_like(acc_ref)
        kscr_ref[0:nq, :] = kn_ref[0]
        vscr_ref[0:nq, :] = vn_ref[0]
        j = lax.broadcasted_iota(jnp.int32, (NEW_ROWS, ncol), 0)
        i = lax.broadcasted_iota(jnp.int32, (NEW_ROWS, ncol), 1) % nq
        blocks_step(kscr_ref[...].astype(BF16), vscr_ref[...].astype(BF16), NEW_ROWS, j < i)

    blocks_step(load_pages(kp_refs), load_pages(vp_refs), page, None)

    @pl.when(p == pl.num_programs(1) - 1)
    def _():
        acc = acc_ref[...]
        rh = lax.broadcasted_iota(jnp.int32, (ncol, d), 0) // nq
        o = jnp.zeros((ncol, d), F32)
        for hh in range(nh):
            o = o + jnp.where(rh == hh, acc[:, hh * d:(hh + 1) * d], 0.0)
        o = jnp.concatenate([o[hh * nq:(hh + 1) * nq, :] for hh in range(nh)], axis=1)
        g = gate_ref[0]
        o_ref[0] = o * (g * _sigmoid(g))


def sb_paged_attn(q, k_new, v_new, gate, bias, cache_k, cache_v, page_table, page_base, *, page, pps):
    nb, nq, width = q.shape
    d = SB_HEAD_DIM
    nh = width // d
    n_pages = page_table.shape[1]
    assert n_pages % pps == 0 and nq <= NEW_ROWS
    ncol = nh * nq
    bias_cols = jnp.repeat(bias.astype(F32), nq).reshape(1, ncol)
    pt = page_table.reshape(-1).astype(jnp.int32) + page_base

    def tok_map(b, p, pt_ref):
        return (b, 0, 0)

    def page_map(slot):
        return lambda b, p, pt_ref: (pt_ref[b * n_pages + (n_pages - 1 - p * pps - slot)], 0)

    page_specs = [pl.BlockSpec((page * nh, d), page_map(s)) for s in range(pps)]
    blocks = 5 * _nbytes((nq, width), F32) + 2 * pps * _nbytes((page, width), F32)
    scratch = (_nbytes((ncol, width), BF16) + 2 * _nbytes((NEW_ROWS, width), F32) + _nbytes((ncol, width), F32))
    grid_spec = pltpu.PrefetchScalarGridSpec(
        num_scalar_prefetch=1,
        grid=(nb, n_pages // pps),
        in_specs=[
            pl.BlockSpec((1, nq, width), tok_map),
            pl.BlockSpec((1, nq, width), tok_map),
            pl.BlockSpec((1, nq, width), tok_map),
            pl.BlockSpec((1, nq, width), tok_map),
            pl.BlockSpec((1, ncol), lambda b, p, pt_ref: (0, 0)),
        ] + page_specs + page_specs,
        out_specs=pl.BlockSpec((1, nq, width), tok_map),
        scratch_shapes=[
            pltpu.VMEM((ncol, width), BF16),
            pltpu.VMEM((NEW_ROWS, width), F32),
            pltpu.VMEM((NEW_ROWS, width), F32),
            pltpu.VMEM((1, ncol), F32),
            pltpu.VMEM((ncol, width), F32),
        ],
    )
    return pl.pallas_call(
        functools.partial(_sb_paged_kernel, nq=nq, nh=nh, page=page, pps=pps),
        grid_spec=grid_spec,
        out_shape=jax.ShapeDtypeStruct((nb, nq, width), F32),
        compiler_params=pltpu.CompilerParams(
            dimension_semantics=("arbitrary", "arbitrary"),
            vmem_limit_bytes=_vmem_limit(blocks, scratch, 3 * pps * _nbytes((page, width), F32)),
        ),
        name="sb_paged_attn",
    )(pt, q, k_new, v_new, gate, bias_cols, *([cache_k] * pps), *([cache_v] * pps))


def _ssd_prep_kernel(dtraw_ref, dtb_ref, alog_ref, e3_ref, dte_ref, ace_ref, ac_ref, *, chunk):
    rows = dtraw_ref.shape[0]
    dt = _softplus(dtraw_ref[...] + dtb_ref[...])
    a = -jnp.exp(alog_ref[...])
    da = dt * a
    q = lax.broadcasted_iota(jnp.int32, (rows, 3 * rows), 0)
    s = lax.broadcasted_iota(jnp.int32, (rows, 3 * rows), 1) % rows
    tri3 = jnp.where((s <= q) & (s // chunk == q // chunk), 1.0, 0.0).astype(BF16)
    acum = _dot(tri3, jnp.concatenate(_split3(da), axis=0))
    ac_ref[...] = acum
    dte_ref[...] = _dot(jnp.concatenate(_split3(dt), axis=1), e3_ref[...])
    ace_ref[...] = _dot(jnp.concatenate(_split3(acum), axis=1), e3_ref[...])


def ssd_prep(dt_raw, dt_bias, a_log, *, chunk, rows):
    m = dt_raw.shape[0]
    nh = dt_bias.shape[0]
    inner = nh * SSD_HEAD_DIM
    assert m % rows == 0 and rows % chunk == 0 and nh <= LANES

    def pad(v):
        return jnp.zeros((1, LANES), F32).at[0, :nh].set(v.astype(F32))

    lane = jnp.arange(3 * LANES) % LANES
    e3 = (lane[:, None] == (jnp.arange(inner) // SSD_HEAD_DIM)[None, :]).astype(BF16)
    blocks = 2 * _nbytes((rows, LANES), F32) + _nbytes((3 * LANES, inner), BF16) + 2 * _nbytes((rows, inner), F32)
    return pl.pallas_call(
        functools.partial(_ssd_prep_kernel, chunk=chunk),
        grid=(m // rows,),
        in_specs=[
            pl.BlockSpec((rows, LANES), lambda i: (i, 0)),
            pl.BlockSpec((1, LANES), lambda i: (0, 0)),
            pl.BlockSpec((1, LANES), lambda i: (0, 0)),
            pl.BlockSpec((3 * LANES, inner), lambda i: (0, 0)),
        ],
        out_specs=[
            pl.BlockSpec((rows, inner), lambda i: (i, 0)),
            pl.BlockSpec((rows, inner), lambda i: (i, 0)),
            pl.BlockSpec((rows, LANES), lambda i: (i, 0)),
        ],
        out_shape=[
            jax.ShapeDtypeStruct((m, inner), F32),
            jax.ShapeDtypeStruct((m, inner), F32),
            jax.ShapeDtypeStruct((m, LANES), F32),
        ],
        compiler_params=pltpu.CompilerParams(
            dimension_semantics=("arbitrary",),
            vmem_limit_bytes=_vmem_limit(blocks, 0, 2 * _nbytes((rows, inner), F32)),
        ),
        name="ssd_prep",
    )(dt_raw, pad(dt_bias), pad(a_log), e3)


HALO = SUBLANES


def _ssd_chunk_kernel(*refs, q_len, n_chunks, nh):
    if n_chunks > 1:
        (z_ref, xbc_ref, halo_ref, cs_ref, dte_ref, ace_ref, ac_ref, s0_ref, cw_ref, cb_ref, dsk_ref, gw_ref,
         y_ref, st_ref, xp_ref) = refs
    else:
        (z_ref, xbc_ref, cs_ref, dte_ref, ace_ref, ac_ref, s0_ref, cw_ref, cb_ref, dsk_ref, gw_ref,
         y_ref, st_ref, xp_ref) = refs
        halo_ref = None
    ci = pl.program_id(1)
    p = SSD_HEAD_DIM
    n = SSD_STATE
    inner = nh * p
    gw_cols = inner // SSD_GROUPS
    hg = nh // SSD_GROUPS
    nprev = SSD_CONV - 1

    @pl.when(ci == 0)
    def _():
        st_ref[...] = s0_ref[...]

    xp_ref[HALO:HALO + q_len, :] = xbc_ref[0]
    if halo_ref is None:
        xp_ref[HALO - nprev:HALO, :] = cs_ref[0]
    else:
        @pl.when(ci == 0)
        def _():
            xp_ref[HALO - nprev:HALO, :] = cs_ref[0]

        @pl.when(ci > 0)
        def _():
            xp_ref[HALO - nprev:HALO, :] = halo_ref[0, HALO - nprev:HALO, :]

    conv = cb_ref[...]
    for j in range(SSD_CONV):
        conv = conv + xp_ref[HALO - nprev + j:HALO - nprev + j + q_len, :] * cw_ref[j:j + 1, :]
    xc = conv * _sigmoid(conv)

    row = lax.broadcasted_iota(jnp.int32, (q_len, q_len), 0)
    col = lax.broadcasted_iota(jnp.int32, (q_len, q_len), 1)
    causal = row >= col
    lane = lax.broadcasted_iota(jnp.int32, (q_len, LANES), 1)
    if q_len >= LANES:
        act = ac_ref[0].T
    ones16 = jnp.where(lax.broadcasted_iota(jnp.int32, (16, n), 0) < 3, 1.0, 0.0).astype(BF16)
    row16 = lax.broadcasted_iota(jnp.int32, (16, gw_cols), 0)

    for g in range(SSD_GROUPS):
        cs0 = g * gw_cols
        xs = xc[:, cs0:cs0 + gw_cols]
        bq = xc[:, inner + g * n:inner + (g + 1) * n]
        cq = xc[:, inner + SSD_GROUPS * n + g * n:inner + SSD_GROUPS * n + (g + 1) * n]
        dte = dte_ref[0, :, cs0:cs0 + gw_cols]
        ace = ace_ref[0, :, cs0:cs0 + gw_cols]
        xdt = xs * dte
        bq16 = bq.astype(BF16)
        cq16 = cq.astype(BF16)
        s_in = st_ref[0, cs0:cs0 + gw_cols, :]

        if q_len >= LANES:
            cb = _dot_nt(cq16, bq16)
            parts = []
            for pi in range(hg // 2):
                ms = []
                for hh in (2 * pi, 2 * pi + 1):
                    hd = g * hg + hh
                    seg = ac_ref[0, :, hd:hd + 1] - act[hd:hd + 1, :]
                    ms.append(cb * jnp.where(causal, jnp.exp(seg), 0.0))
                lhs = jnp.concatenate(ms, axis=1).astype(BF16)
                xpair = xdt[:, pi * LANES:(pi + 1) * LANES]
                rhs = jnp.concatenate([jnp.where(lane < p, xpair, 0.0), jnp.where(lane >= p, xpair, 0.0)],
                                      axis=0).astype(BF16)
                parts.append(_dot(lhs, rhs))
            y = jnp.concatenate(parts, axis=1)
        else:
            rowq = lax.broadcasted_iota(jnp.int32, (q_len, gw_cols), 0)
            y = jnp.zeros((q_len, gw_cols), F32)
            for s in range(q_len):
                cb_s = jnp.sum(cq * bq[s:s + 1, :], axis=1, keepdims=True)
                term = cb_s * jnp.exp(ace - ace[s:s + 1, :]) * xdt[s:s + 1, :]
                y = y + jnp.where(rowq >= s, term, 0.0)

        y = y + _dot_nt(cq16, s_in.astype(BF16)) * jnp.exp(ace)
        ace_last = ace[q_len - 1:q_len, :]
        xw = xdt * jnp.exp(ace_last - ace)
        pad_rows = (-q_len) % 16
        if pad_rows:
            xw = jnp.concatenate([xw, jnp.zeros((pad_rows, gw_cols), F32)], axis=0)
            bqp = jnp.concatenate([bq, jnp.zeros((pad_rows, n), F32)], axis=0).astype(BF16)
        else:
            bqp = bq16
        new_states = _dot_tn(xw.astype(BF16), bqp)
        d_hi, d_mid, d_lo = (t.astype(F32) for t in _split3(jnp.exp(ace_last)))
        dec3 = jnp.where(row16 == 0, d_hi, jnp.where(row16 == 1, d_mid, jnp.where(row16 == 2, d_lo, 0.0)))
        dec_col = _dot_tn(dec3.astype(BF16), ones16)
        st_ref[0, cs0:cs0 + gw_cols, :] = s_in * dec_col + new_states

        y = (y + dsk_ref[:, cs0:cs0 + gw_cols] * xs)
        zz = z_ref[0, :, cs0:cs0 + gw_cols]
        y = y * (zz * _sigmoid(zz))
        ms = jnp.mean(y * y, axis=-1, keepdims=True)
        y = y * lax.rsqrt(ms + RMS_EPS) * gw_ref[:, cs0:cs0 + gw_cols]
        y_ref[0, :, cs0:cs0 + gw_cols] = y.astype(y_ref.dtype)


def ssd_chunk(z, xbc, conv_state, dt_e, acum_e, acum, ssm_state, conv_w, conv_b, d_skip, gnorm_w, *, q_len):
    nb, l, inner = z.shape
    conv_dim = xbc.shape[2]
    nh = inner // SSD_HEAD_DIM
    n = SSD_STATE
    assert l % q_len == 0
    n_chunks = l // q_len
    assert q_len == LANES or (q_len < 16 and n_chunks == 1)
    y_dtype = BF16 if q_len >= 16 else F32
    dsk = jnp.repeat(d_skip.astype(F32), SSD_HEAD_DIM).reshape(1, inner)

    def seq_chunk(b, c):
        return (b, c, 0)

    def seq_only(b, c):
        return (b, 0, 0)

    def const2(b, c):
        return (0, 0)

    in_specs = [pl.BlockSpec((1, q_len, inner), seq_chunk), pl.BlockSpec((1, q_len, conv_dim), seq_chunk)]
    args = [z, xbc]
    if n_chunks > 1:
        per = q_len // HALO
        in_specs.append(pl.BlockSpec((1, HALO, conv_dim), lambda b, c: (b, jnp.maximum(c * per - 1, 0), 0)))
        args.append(xbc)
    in_specs += [
        pl.BlockSpec((1, SSD_CONV - 1, conv_dim), seq_only),
        pl.BlockSpec((1, q_len, inner), seq_chunk),
        pl.BlockSpec((1, q_len, inner), seq_chunk),
        pl.BlockSpec((1, q_len, LANES), seq_chunk),
        pl.BlockSpec((1, inner, n), seq_only),
        pl.BlockSpec((SSD_CONV, conv_dim), const2),
        pl.BlockSpec((1, conv_dim), const2),
        pl.BlockSpec((1, inner), const2),
        pl.BlockSpec((1, inner), const2),
    ]
    args += [conv_state, dt_e, acum_e, acum, ssm_state, conv_w, conv_b.reshape(1, conv_dim), dsk,
             gnorm_w.reshape(1, inner)]
    qp = -(-q_len // SUBLANES) * SUBLANES
    blocks = (4 * _nbytes((q_len, inner), F32) + _nbytes((q_len, conv_dim), F32) + _nbytes((HALO, conv_dim), F32)
              + 2 * _nbytes((inner, n), F32) + _nbytes((q_len, LANES), F32))
    scratch = _nbytes((HALO + qp, conv_dim), F32)
    return pl.pallas_call(
        functools.partial(_ssd_chunk_kernel, q_len=q_len, n_chunks=n_chunks, nh=nh),
        grid=(nb, n_chunks),
        in_specs=in_specs,
        out_specs=[pl.BlockSpec((1, q_len, inner), seq_chunk), pl.BlockSpec((1, inner, n), seq_only)],
        out_shape=[jax.ShapeDtypeStruct((nb, l, inner), y_dtype), jax.ShapeDtypeStruct((nb, inner, n), F32)],
        scratch_shapes=[pltpu.VMEM((HALO + qp, conv_dim), F32)],
        compiler_params=pltpu.CompilerParams(
            dimension_semantics=("arbitrary", "arbitrary"),
            vmem_limit_bytes=_vmem_limit(blocks, scratch, 6 * _nbytes((max(q_len, SUBLANES), conv_dim), F32)),
        ),
        name="ssd_chunk",
    )(*args)


def _row_block(m, target):
    return target if m % target == 0 else m


def _trunk(x, paged, conv_states, ssm_states, p):
    nb, l, dm = x.shape
    m = nb * l
    xf = x.reshape(m, dm)
    bm = _row_block(m, 1024)
    bm_out = _row_block(m, 512)
    ks, vs, convs, ssms = [], [], [], []
    depth = p["depth"]
    for i in range(depth):
        j = i // 2
        if i % 2 == 0:
            w_in, w_out = p["w_in_sb"][j], p["w_out_sb"][j]
            width = w_out.shape[0]
            scale = SB_HEAD_DIM ** -0.5
            proj = functools.partial(norm_matmul, xf, p["norm_sb"][j], w_in, n_cols=width, bm=bm, bn=512)
            gate = proj(col_start=3 * width, out_dtypes=(F32,))
            if paged is None:
                k, k16 = proj(col_start=width, out_dtypes=(F32, BF16))
                v, v16 = proj(col_start=2 * width, out_dtypes=(F32, BF16))
                q = proj(col_start=0, out_dtypes=(BF16,), out_scale=scale * LOG2E)
                og = jnp.concatenate([
                    sb_prompt_attn(q[b * l:(b + 1) * l], k16[b * l:(b + 1) * l], v16[b * l:(b + 1) * l],
                                   gate[b * l:(b + 1) * l], p["sb_bias"][j], tq=_row_block(l, 1024), tk=256)
                    for b in range(nb)], axis=0)
            else:
                cache_k, cache_v, page_table, n_phys, page = paged
                k = proj(col_start=width, out_dtypes=(F32,))
                v = proj(col_start=2 * width, out_dtypes=(F32,))
                q = proj(col_start=0, out_dtypes=(F32,), out_scale=scale)
                og = sb_paged_attn(q.reshape(nb, l, width), k.reshape(nb, l, width), v.reshape(nb, l, width),
                                   gate.reshape(nb, l, width), p["sb_bias"][j], cache_k, cache_v, page_table,
                                   j * n_phys, page=page,
                                   pps=max(c for c in (4, 2, 1) if page_table.shape[1] % c == 0)).reshape(m, width)
            xf = matmul_res(og, w_out, xf, bm=bm_out, bk=width)
            nh = width // SB_HEAD_DIM
            ks.append(k.reshape(nb, l, nh, SB_HEAD_DIM))
            vs.append(v.reshape(nb, l, nh, SB_HEAD_DIM))
        else:
            w_in, w_dt, w_out = p["w_in_ssd"][j], p["w_dt_ssd"][j], p["w_out_ssd"][j]
            inner = w_out.shape[0]
            conv_dim = p["conv_w"].shape[2]
            nh = inner // SSD_HEAD_DIM
            proj = functools.partial(norm_matmul, xf, p["norm_ssd"][j], bm=bm)
            z = proj(w_in, col_start=0, n_cols=inner, out_dtypes=(F32,), bn=512)
            xbc = proj(w_in, col_start=inner, n_cols=conv_dim, out_dtypes=(F32,), bn=512)
            dt_raw = proj(w_dt, col_start=0, n_cols=LANES, out_dtypes=(F32,), bn=LANES)
            q_len = SSD_CHUNK if l % SSD_CHUNK == 0 else l
            dt_e, acum_e, acum = ssd_prep(dt_raw, p["dt_bias"][j], p["a_log"][j], chunk=q_len,
                                          rows=_row_block(m, LANES))
            y, new_ssm = ssd_chunk(
                z.reshape(nb, l, inner), xbc.reshape(nb, l, conv_dim), conv_states[j],
                dt_e.reshape(nb, l, inner), acum_e.reshape(nb, l, inner), acum.reshape(nb, l, LANES),
                ssm_states[j].reshape(nb, inner, SSD_STATE), p["conv_w"][j], p["conv_b"][j], p["d_skip"][j],
                p["gnorm_w"][j], q_len=q_len)
            last = i == depth - 1
            xf = matmul_res(y.reshape(m, inner), w_out, xf, p["norm_f"] if last else None,
                            bm=_row_block(m, 256), bk=inner)
            xbc3 = xbc.reshape(nb, l, conv_dim)
            nprev = SSD_CONV - 1
            if l >= nprev:
                convs.append(xbc3[:, l - nprev:])
            else:
                convs.append(jnp.concatenate([conv_states[j].astype(F32), xbc3], axis=1)[:, l:])
            ssms.append(new_ssm.reshape(nb, nh, SSD_HEAD_DIM, SSD_STATE))
    if depth % 2 == 1:
        raise NotImplementedError("final norm is fused into the last SSD output projection")
    return xf.reshape(nb, l, dm), jnp.stack(ks), jnp.stack(vs), jnp.stack(convs), jnp.stack(ssms)


def kernel(x_prompt, x_sample, cache_k, cache_v, state_conv, state_ssm, page_table, norm_sb, w_in_sb, w_out_sb,
           sb_bias, norm_ssd, w_in_ssd, conv_w, conv_b, dt_bias, a_log, d_skip, gnorm_w, w_out_ssd, norm_f):
    n_sb = w_in_sb.shape[0]
    n_ssd = w_in_ssd.shape[0]
    inner = w_out_ssd.shape[1]
    conv_dim = conv_w.shape[2]
    nh_ssd = dt_bias.shape[1]
    dt_cols = w_in_ssd[:, :, inner + conv_dim:inner + conv_dim + nh_ssd]
    w_dt = jnp.zeros((n_ssd, w_in_ssd.shape[1], LANES), F32).at[:, :, :nh_ssd].set(dt_cols)
    p = dict(
        depth=n_sb + n_ssd,
        norm_sb=norm_sb, w_in_sb=w_in_sb.astype(BF16), w_out_sb=w_out_sb.astype(BF16), sb_bias=sb_bias,
        norm_ssd=norm_ssd, w_in_ssd=w_in_ssd[:, :, :inner + conv_dim].astype(BF16), w_dt_ssd=w_dt.astype(BF16),
        w_out_ssd=w_out_ssd.astype(BF16), conv_w=conv_w, conv_b=conv_b, dt_bias=dt_bias, a_log=a_log,
        d_skip=d_skip, gnorm_w=gnorm_w, norm_f=norm_f,
    )
    nbp = x_prompt.shape[0]
    zero_conv = jnp.zeros((n_ssd, nbp, SSD_CONV - 1, conv_dim), x_prompt.dtype)
    zero_ssm = jnp.zeros((n_ssd, nbp, nh_ssd, SSD_HEAD_DIM, SSD_STATE), x_prompt.dtype)
    y_p, k_p, v_p, conv_p, ssm_p = _trunk(x_prompt, None, zero_conv, zero_ssm, p)

    n_phys, page = cache_k.shape[1], cache_k.shape[2]
    paged = (cache_k.reshape(-1, cache_k.shape[4]), cache_v.reshape(-1, cache_v.shape[4]), page_table, n_phys, page)
    y_s, k_s, v_s, conv_s, ssm_s = _trunk(x_sample, paged, state_conv, state_ssm, p)
    return (y_p, y_s, k_p, v_p, conv_p, ssm_p, k_s, v_s, conv_s, ssm_s)
```

```python
import functools

import jax
import jax.numpy as jnp
from jax import lax
from jax.experimental import pallas as pl
from jax.experimental.pallas import tpu as pltpu

F32 = jnp.float32
BF16 = jnp.bfloat16
RMS_EPS = 1e-6

SB_HEAD_DIM = 128
SSD_HEAD_DIM = 64
SSD_GROUPS = 8
SSD_STATE = 128
SSD_CONV = 4
SSD_CHUNK = 128

LANES = 128
SUBLANES = 8
VMEM_PHYSICAL_BYTES = 64 * 1024 * 1024
VMEM_LIMIT_CAP_BYTES = 56 * 1024 * 1024


def _vmem_limit(block_bytes, scratch_bytes=0, temp_bytes=0):
    est = 2 * block_bytes + scratch_bytes + temp_bytes
    return int(min(max(est, 16 * 1024 * 1024), VMEM_LIMIT_CAP_BYTES))


def _nbytes(shape, dtype):
    n = 1
    for s in shape:
        n *= s
    return n * jnp.dtype(dtype).itemsize


def _sigmoid(x):
    return 1.0 / (1.0 + jnp.exp(-x))


def _softplus(x):
    return jnp.maximum(x, 0.0) + jnp.log(1.0 + jnp.exp(-jnp.abs(x)))


def _split2(x):
    hi = x.astype(BF16)
    lo = (x - hi.astype(F32)).astype(BF16)
    return hi, lo


def _split3(x):
    hi = x.astype(BF16)
    r1 = x - hi.astype(F32)
    mid = r1.astype(BF16)
    lo = (r1 - mid.astype(F32)).astype(BF16)
    return hi, mid, lo


def _dot(a, b):
    return jnp.dot(a, b, preferred_element_type=F32)


def _dot_nt(a, b):
    return lax.dot_general(a, b, (((1,), (1,)), ((), ())), preferred_element_type=F32)


def _dot_tn(a, b):
    return lax.dot_general(a, b, (((0,), (0,)), ((), ())), preferred_element_type=F32)


def _norm_matmul_kernel(x_ref, g_ref, w_ref, *rest, out_scale):
    o_refs, xn_ref = rest[:-1], rest[-1]

    @pl.when(pl.program_id(1) == 0)
    def _():
        x = x_ref[...]
        ms = jnp.mean(x * x, axis=-1, keepdims=True)
        xn_ref[...] = (x * lax.rsqrt(ms + RMS_EPS) * g_ref[...]).astype(BF16)

    acc = _dot(xn_ref[...], w_ref[...])
    if out_scale != 1.0:
        acc = acc * out_scale
    for o_ref in o_refs:
        o_ref[...] = acc.astype(o_ref.dtype)


def norm_matmul(x, g, w, *, col_start, n_cols, out_dtypes, out_scale=1.0, bm, bn):
    m, k = x.shape
    assert m % bm == 0 and n_cols % bn == 0 and col_start % bn == 0
    off = col_start // bn
    blocks = _nbytes((bm, k), F32) + _nbytes((k, bn), BF16) + sum(_nbytes((bm, bn), dt) for dt in out_dtypes)
    outs = pl.pallas_call(
        functools.partial(_norm_matmul_kernel, out_scale=out_scale),
        grid=(m // bm, n_cols // bn),
        in_specs=[
            pl.BlockSpec((bm, k), lambda i, j: (i, 0)),
            pl.BlockSpec((1, k), lambda i, j: (0, 0)),
            pl.BlockSpec((k, bn), lambda i, j: (0, j + off)),
        ],
        out_specs=[pl.BlockSpec((bm, bn), lambda i, j: (i, j)) for _ in out_dtypes],
        out_shape=[jax.ShapeDtypeStruct((m, n_cols), dt) for dt in out_dtypes],
        scratch_shapes=[pltpu.VMEM((bm, k), BF16)],
        compiler_params=pltpu.CompilerParams(
            dimension_semantics=("arbitrary", "arbitrary"),
            vmem_limit_bytes=_vmem_limit(blocks, _nbytes((bm, k), BF16), 2 * _nbytes((bm, k), F32)),
        ),
        name="norm_matmul",
    )(x, g.reshape(1, k), w)
    return outs[0] if len(outs) == 1 else tuple(outs)


def _matmul_res_kernel(a_ref, w_ref, r_ref, g_ref, o_ref, *, final_norm):
    kk = pl.program_id(1)

    @pl.when(kk == 0)
    def _():
        o_ref[...] = r_ref[...]

    o_ref[...] += _dot(a_ref[...].astype(BF16), w_ref[...])

    if final_norm:
        @pl.when(kk == pl.num_programs(1) - 1)
        def _():
            x = o_ref[...]
            ms = jnp.mean(x * x, axis=-1, keepdims=True)
            o_ref[...] = x * lax.rsqrt(ms + RMS_EPS) * g_ref[...]


def matmul_res(a, w, res, g=None, *, bm, bk):
    m, k = a.shape
    n = w.shape[1]
    assert m % bm == 0 and k % bk == 0
    final_norm = g is not None
    if g is None:
        g = jnp.ones((n,), F32)
    blocks = _nbytes((bm, bk), a.dtype) + _nbytes((bk, n), BF16) + 2 * _nbytes((bm, n), F32)
    return pl.pallas_call(
        functools.partial(_matmul_res_kernel, final_norm=final_norm),
        grid=(m // bm, k // bk),
        in_specs=[
            pl.BlockSpec((bm, bk), lambda i, j: (i, j)),
            pl.BlockSpec((bk, n), lambda i, j: (j, 0)),
            pl.BlockSpec((bm, n), lambda i, j: (i, 0)),
            pl.BlockSpec((1, n), lambda i, j: (0, 0)),
        ],
        out_specs=pl.BlockSpec((bm, n), lambda i, j: (i, 0)),
        out_shape=jax.ShapeDtypeStruct((m, n), F32),
        compiler_params=pltpu.CompilerParams(
            dimension_semantics=("arbitrary", "arbitrary"),
            vmem_limit_bytes=_vmem_limit(blocks, 0, 2 * _nbytes((bm, n), F32)),
        ),
        name="matmul_res",
    )(a, w, res, g.reshape(1, n))


def _neg_softplus(z):
    return -_softplus(z)


LOG2E = 1.4426950408889634
MASKED_LOG = -1e30


def _sb_prompt_kernel(bias_ref, q_ref, k_ref, v_ref, gate_ref, o_ref, lb_ref, lk_ref, lk0_ref, carry_ref, acc_ref,
                      *, tq, tk):
    h = pl.program_id(0)
    qi = pl.program_id(1)
    bias2 = bias_ref[h] * LOG2E
    nd = tq // tk

    def stage1(kb, slot, masked):
        start = pl.multiple_of(kb * tk, tk)
        z = _dot_nt(q_ref[...], k_ref[pl.ds(start, tk), :]) + bias2
        lb = jnp.minimum(z, 0.0) - jnp.log2(1.0 + jnp.exp2(-jnp.abs(z)))
        lk = lb - z
        if masked:
            t_pos = qi * tq + lax.broadcasted_iota(jnp.int32, (tq, tk), 0)
            s_pos = kb * tk + lax.broadcasted_iota(jnp.int32, (tq, tk), 1)
            lk = jnp.where(s_pos < t_pos, lk, 0.0)
            lb = jnp.where(s_pos < t_pos, lb, MASKED_LOG)
        lb_ref[slot] = lb
        lk_ref[slot] = lk.astype(BF16)
        lk0_ref[slot] = lk[:, 0:1]

    def stage2(kb, slot):
        start = pl.multiple_of(kb * tk, tk)
        r = lax.broadcasted_iota(jnp.int32, (tk, tk), 0)
        c = lax.broadcasted_iota(jnp.int32, (tk, tk), 1)
        tri = jnp.where(r > c, 1.0, 0.0).astype(BF16)
        mm = _dot(lk_ref[slot], tri)
        w = jnp.exp2(lb_ref[slot] + mm)
        carry = carry_ref[...]
        acc_ref[...] += _dot(w.astype(BF16), v_ref[pl.ds(start, tk), :]) * jnp.exp2(carry)
        carry_ref[...] = carry + mm[:, 0:1] + lk0_ref[slot]

    carry_ref[...] = jnp.zeros_like(carry_ref)
    acc_ref[...] = jnp.zeros_like(acc_ref)
    kb_top = qi * nd + nd - 1
    stage1(kb_top, 0, True)
    for dd in range(1, nd):
        stage1(kb_top - dd, dd % 2, True)
        stage2(kb_top - dd + 1, (dd - 1) % 2)
    pending = (nd - 1) % 2

    def body(i, _):
        kb = qi * nd - 1 - 2 * i
        stage1(kb, 1 - pending, False)
        stage2(kb + 1, pending)
        stage1(kb - 1, pending, False)
        stage2(kb, 1 - pending)
        return 0

    lax.fori_loop(0, (qi * nd) // 2, body, 0)
    stage2(0, pending)
    g = gate_ref[...]
    o_ref[...] = (acc_ref[...] * (g * _sigmoid(g))).astype(o_ref.dtype)


def sb_prompt_attn(q, k, v, gate, bias, *, tq, tk):
    l, width = q.shape
    d = SB_HEAD_DIM
    nh = width // d
    assert l % tq == 0 and tq % (2 * tk) == 0
    blocks = _nbytes((tq, d), BF16) * 2 + 2 * _nbytes((l, d), BF16) + _nbytes((tq, d), F32)
    scratch_shapes = [
        pltpu.VMEM((2, tq, tk), F32),
        pltpu.VMEM((2, tq, tk), BF16),
        pltpu.VMEM((2, tq, 1), F32),
        pltpu.VMEM((tq, 1), F32),
        pltpu.VMEM((tq, d), F32),
    ]
    scratch = 2 * _nbytes((tq, tk), F32) + 2 * _nbytes((tq, tk), BF16) + 4 * _nbytes((tq, LANES), F32)
    return pl.pallas_call(
        functools.partial(_sb_prompt_kernel, tq=tq, tk=tk),
        grid=(nh, l // tq),
        in_specs=[
            pl.BlockSpec(memory_space=pltpu.SMEM),
            pl.BlockSpec((tq, d), lambda h, i: (i, h)),
            pl.BlockSpec((l, d), lambda h, i: (0, h)),
            pl.BlockSpec((l, d), lambda h, i: (0, h)),
            pl.BlockSpec((tq, d), lambda h, i: (i, h)),
        ],
        out_specs=pl.BlockSpec((tq, d), lambda h, i: (i, h)),
        out_shape=jax.ShapeDtypeStruct((l, width), BF16),
        scratch_shapes=scratch_shapes,
        compiler_params=pltpu.CompilerParams(
            dimension_semantics=("arbitrary", "arbitrary"),
            vmem_limit_bytes=_vmem_limit(blocks, scratch, 12 * _nbytes((tq, tk), F32)),
        ),
        name="sb_prompt_attn",
    )(bias, q, k, v, gate)


NEW_ROWS = 16


def _sb_paged_kernel(pt_ref, q_ref, kn_ref, vn_ref, gate_ref, bias_ref, *rest, nq, nh, page, pps):
    kp_refs, vp_refs = rest[:pps], rest[pps:2 * pps]
    o_ref, qrows_ref, kscr_ref, vscr_ref, carry_ref, acc_ref = rest[2 * pps:]
    b = pl.program_id(0)
    p = pl.program_id(1)
    d = SB_HEAD_DIM
    width = nh * d
    ncol = nh * nq

    def suffix_tri(rows):
        r = lax.broadcasted_iota(jnp.int32, (rows, 2 * rows), 0)
        c = lax.broadcasted_iota(jnp.int32, (rows, 2 * rows), 1)
        return jnp.where(jnp.where(c >= rows, c - rows, c) >= r, 1.0, 0.0).astype(BF16)

    def blocks_step(k16, v16, rows, mask):
        n = k16.shape[0] // rows
        tri2 = suffix_tri(rows)
        z = _dot_nt(k16, qrows_ref[...]) + bias_ref[...]
        lk = _neg_softplus(z)
        if mask is not None:
            lk = jnp.where(mask, lk, 0.0)
        hi, lo = _split2(lk)
        carry = carry_ref[...]
        ws = []
        for i in range(n):
            sl = slice(i * rows, (i + 1) * rows)
            cs = _dot(tri2, jnp.concatenate([hi[sl], lo[sl]], axis=0)) + carry
            ws.append(jnp.exp(z[sl] + cs))
            carry = cs[0:1, :]
        w = ws[0] if n == 1 else jnp.concatenate(ws, axis=0)
        if mask is not None:
            w = jnp.where(mask, w, 0.0)
        acc_ref[...] += _dot_tn(w.astype(BF16), v16)
        carry_ref[...] = carry

    def load_pages(refs):
        pages = [jnp.concatenate([ref[pl.ds(hh, page, stride=nh), :] for hh in range(nh)], axis=1).astype(BF16)
                 for ref in refs]
        return pages[0] if pps == 1 else jnp.concatenate(pages, axis=0)

    @pl.when((b == 0) & (p == 0))
    def _():
        kscr_ref[...] = jnp.zeros_like(kscr_ref)
        vscr_ref[...] = jnp.zeros_like(vscr_ref)

    @pl.when(p == 0)
    def _():
        qt = jnp.concatenate([q_ref[0]] * nh, axis=0)
        rh = lax.broadcasted_iota(jnp.int32, (ncol, width), 0) // nq
        ch = lax.broadcasted_iota(jnp.int32, (ncol, width), 1) // d
        qrows_ref[...] = jnp.where(rh == ch, qt, 0.0).astype(BF16)
        carry_ref[...] = jnp.zeros_like(carry_ref)
        acc_ref[...] = jnp.zeros_like(acc_ref)
        kscr_ref[0:nq, :] = kn_ref[0]
        vscr_ref[0:nq, :] = vn_ref[0]
        j = lax.broadcasted_iota(jnp.int32, (NEW_ROWS, ncol), 0)
        i = lax.broadcasted_iota(jnp.int32, (NEW_ROWS, ncol), 1) % nq
        blocks_step(kscr_ref[...].astype(BF16), vscr_ref[...].astype(BF16), NEW_ROWS, j < i)

    blocks_step(load_pages(kp_refs), load_pages(vp_refs), page, None)

    @pl.when(p == pl.num_programs(1) - 1)
    def _():
        acc = acc_ref[...]
        rh = lax.broadcasted_iota(jnp.int32, (ncol, d), 0) // nq
        o = jnp.zeros((ncol, d), F32)
        for hh in range(nh):
            o = o + jnp.where(rh == hh, acc[:, hh * d:(hh + 1) * d], 0.0)
        o = jnp.concatenate([o[hh * nq:(hh + 1) * nq, :] for hh in range(nh)], axis=1)
        g = gate_ref[0]
        o_ref[0] = o * (g * _sigmoid(g))


def sb_paged_attn(q, k_new, v_new, gate, bias, cache_k, cache_v, page_table, page_base, *, page, pps):
    nb, nq, width = q.shape
    d = SB_HEAD_DIM
    nh = width // d
    n_pages = page_table.shape[1]
    assert n_pages % pps == 0 and nq <= NEW_ROWS
    ncol = nh * nq
    bias_cols = jnp.repeat(bias.astype(F32), nq).reshape(1, ncol)
    pt = page_table.reshape(-1).astype(jnp.int32) + page_base

    def tok_map(b, p, pt_ref):
        return (b, 0, 0)

    def page_map(slot):
        return lambda b, p, pt_ref: (pt_ref[b * n_pages + (n_pages - 1 - p * pps - slot)], 0)

    page_specs = [pl.BlockSpec((page * nh, d), page_map(s)) for s in range(pps)]
    blocks = 5 * _nbytes((nq, width), F32) + 2 * pps * _nbytes((page, width), F32)
    scratch = (_nbytes((ncol, width), BF16) + 2 * _nbytes((NEW_ROWS, width), F32) + _nbytes((ncol, width), F32))
    grid_spec = pltpu.PrefetchScalarGridSpec(
        num_scalar_prefetch=1,
        grid=(nb, n_pages // pps),
        in_specs=[
            pl.BlockSpec((1, nq, width), tok_map),
            pl.BlockSpec((1, nq, width), tok_map),
            pl.BlockSpec((1, nq, width), tok_map),
            pl.BlockSpec((1, nq, width), tok_map),
            pl.BlockSpec((1, ncol), lambda b, p, pt_ref: (0, 0)),
        ] + page_specs + page_specs,
        out_specs=pl.BlockSpec((1, nq, width), tok_map),
        scratch_shapes=[
            pltpu.VMEM((ncol, width), BF16),
            pltpu.VMEM((NEW_ROWS, width), F32),
            pltpu.VMEM((NEW_ROWS, width), F32),
            pltpu.VMEM((1, ncol), F32),
            pltpu.VMEM((ncol, width), F32),
        ],
    )
    return pl.pallas_call(
        functools.partial(_sb_paged_kernel, nq=nq, nh=nh, page=page, pps=pps),
        grid_spec=grid_spec,
        out_shape=jax.ShapeDtypeStruct((nb, nq, width), F32),
        compiler_params=pltpu.CompilerParams(
            dimension_semantics=("arbitrary", "arbitrary"),
            vmem_limit_bytes=_vmem_limit(blocks, scratch, 3 * pps * _nbytes((page, width), F32)),
        ),
        name="sb_paged_attn",
    )(pt, q, k_new, v_new, gate, bias_cols, *([cache_k] * pps), *([cache_v] * pps))


def _ssd_prep_kernel(dtraw_ref, dtb_ref, alog_ref, e3_ref, dte_ref, ace_ref, ac_ref, *, chunk):
    rows = dtraw_ref.shape[0]
    dt = _softplus(dtraw_ref[...] + dtb_ref[...])
    a = -jnp.exp(alog_ref[...])
    da = dt * a
    q = lax.broadcasted_iota(jnp.int32, (rows, 3 * rows), 0)
    s = lax.broadcasted_iota(jnp.int32, (rows, 3 * rows), 1) % rows
    tri3 = jnp.where((s <= q) & (s // chunk == q // chunk), 1.0, 0.0).astype(BF16)
    acum = _dot(tri3, jnp.concatenate(_split3(da), axis=0))
    ac_ref[...] = acum
    dte_ref[...] = _dot(jnp.concatenate(_split3(dt), axis=1), e3_ref[...])
    ace_ref[...] = _dot(jnp.concatenate(_split3(acum), axis=1), e3_ref[...])


def ssd_prep(dt_raw, dt_bias, a_log, *, chunk, rows):
    m = dt_raw.shape[0]
    nh = dt_bias.shape[0]
    inner = nh * SSD_HEAD_DIM
    assert m % rows == 0 and rows % chunk == 0 and nh <= LANES

    def pad(v):
        return jnp.zeros((1, LANES), F32).at[0, :nh].set(v.astype(F32))

    lane = jnp.arange(3 * LANES) % LANES
    e3 = (lane[:, None] == (jnp.arange(inner) // SSD_HEAD_DIM)[None, :]).astype(BF16)
    blocks = 2 * _nbytes((rows, LANES), F32) + _nbytes((3 * LANES, inner), BF16) + 2 * _nbytes((rows, inner), F32)
    return pl.pallas_call(
        functools.partial(_ssd_prep_kernel, chunk=chunk),
        grid=(m // rows,),
        in_specs=[
            pl.BlockSpec((rows, LANES), lambda i: (i, 0)),
            pl.BlockSpec((1, LANES), lambda i: (0, 0)),
            pl.BlockSpec((1, LANES), lambda i: (0, 0)),
            pl.BlockSpec((3 * LANES, inner), lambda i: (0, 0)),
        ],
        out_specs=[
            pl.BlockSpec((rows, inner), lambda i: (i, 0)),
            pl.BlockSpec((rows, inner), lambda i: (i, 0)),
            pl.BlockSpec((rows, LANES), lambda i: (i, 0)),
        ],
        out_shape=[
            jax.ShapeDtypeStruct((m, inner), F32),
            jax.ShapeDtypeStruct((m, inner), F32),
            jax.ShapeDtypeStruct((m, LANES), F32),
        ],
        compiler_params=pltpu.CompilerParams(
            dimension_semantics=("arbitrary",),
            vmem_limit_bytes=_vmem_limit(blocks, 0, 2 * _nbytes((rows, inner), F32)),
        ),
        name="ssd_prep",
    )(dt_raw, pad(dt_bias), pad(a_log), e3)


HALO = SUBLANES


def _ssd_chunk_kernel(*refs, q_len, n_chunks, nh):
    if n_chunks > 1:
        (z_ref, xbc_ref, halo_ref, cs_ref, dte_ref, ace_ref, ac_ref, s0_ref, cw_ref, cb_ref, dsk_ref, gw_ref,
         y_ref, st_ref, xp_ref) = refs
    else:
        (z_ref, xbc_ref, cs_ref, dte_ref, ace_ref, ac_ref, s0_ref, cw_ref, cb_ref, dsk_ref, gw_ref,
         y_ref, st_ref, xp_ref) = refs
        halo_ref = None
    ci = pl.program_id(1)
    p = SSD_HEAD_DIM
    n = SSD_STATE
    inner = nh * p
    gw_cols = inner // SSD_GROUPS
    hg = nh // SSD_GROUPS
    nprev = SSD_CONV - 1

    @pl.when(ci == 0)
    def _():
        st_ref[...] = s0_ref[...]

    xp_ref[HALO:HALO + q_len, :] = xbc_ref[0]
    if halo_ref is None:
        xp_ref[HALO - nprev:HALO, :] = cs_ref[0]
    else:
        @pl.when(ci == 0)
        def _():
            xp_ref[HALO - nprev:HALO, :] = cs_ref[0]

        @pl.when(ci > 0)
        def _():
            xp_ref[HALO - nprev:HALO, :] = halo_ref[0, HALO - nprev:HALO, :]

    conv = cb_ref[...]
    for j in range(SSD_CONV):
        conv = conv + xp_ref[HALO - nprev + j:HALO - nprev + j + q_len, :] * cw_ref[j:j + 1, :]
    xc = conv * _sigmoid(conv)

    row = lax.broadcasted_iota(jnp.int32, (q_len, q_len), 0)
    col = lax.broadcasted_iota(jnp.int32, (q_len, q_len), 1)
    causal = row >= col
    lane = lax.broadcasted_iota(jnp.int32, (q_len, LANES), 1)
    if q_len >= LANES:
        act = ac_ref[0].T
    ones16 = jnp.where(lax.broadcasted_iota(jnp.int32, (16, n), 0) < 3, 1.0, 0.0).astype(BF16)
    row16 = lax.broadcasted_iota(jnp.int32, (16, gw_cols), 0)

    for g in range(SSD_GROUPS):
        cs0 = g * gw_cols
        xs = xc[:, cs0:cs0 + gw_cols]
        bq = xc[:, inner + g * n:inner + (g + 1) * n]
        cq = xc[:, inner + SSD_GROUPS * n + g * n:inner + SSD_GROUPS * n + (g + 1) * n]
        dte = dte_ref[0, :, cs0:cs0 + gw_cols]
        ace = ace_ref[0, :, cs0:cs0 + gw_cols]
        xdt = xs * dte
        bq16 = bq.astype(BF16)
        cq16 = cq.astype(BF16)
        s_in = st_ref[0, cs0:cs0 + gw_cols, :]

        if q_len >= LANES:
            cb = _dot_nt(cq16, bq16)
            parts = []
            for pi in range(hg // 2):
                ms = []
                for hh in (2 * pi, 2 * pi + 1):
                    hd = g * hg + hh
                    seg = ac_ref[0, :, hd:hd + 1] - act[hd:hd + 1, :]
                    ms.append(cb * jnp.where(causal, jnp.exp(seg), 0.0))
                lhs = jnp.concatenate(ms, axis=1).astype(BF16)
                xpair = xdt[:, pi * LANES:(pi + 1) * LANES]
                rhs = jnp.concatenate([jnp.where(lane < p, xpair, 0.0), jnp.where(lane >= p, xpair, 0.0)],
                                      axis=0).astype(BF16)
                parts.append(_dot(lhs, rhs))
            y = jnp.concatenate(parts, axis=1)
        else:
            rowq = lax.broadcasted_iota(jnp.int32, (q_len, gw_cols), 0)
            y = jnp.zeros((q_len, gw_cols), F32)
            for s in range(q_len):
                cb_s = jnp.sum(cq * bq[s:s + 1, :], axis=1, keepdims=True)
                term = cb_s * jnp.exp(ace - ace[s:s + 1, :]) * xdt[s:s + 1, :]
                y = y + jnp.where(rowq >= s, term, 0.0)

        y = y + _dot_nt(cq16, s_in.astype(BF16)) * jnp.exp(ace)
        ace_last = ace[q_len - 1:q_len, :]
        xw = xdt * jnp.exp(ace_last - ace)
        pad_rows = (-q_len) % 16
        if pad_rows:
            xw = jnp.concatenate([xw, jnp.zeros((pad_rows, gw_cols), F32)], axis=0)
            bqp = jnp.concatenate([bq, jnp.zeros((pad_rows, n), F32)], axis=0).astype(BF16)
        else:
            bqp = bq16
        new_states = _dot_tn(xw.astype(BF16), bqp)
        d_hi, d_mid, d_lo = (t.astype(F32) for t in _split3(jnp.exp(ace_last)))
        dec3 = jnp.where(row16 == 0, d_hi, jnp.where(row16 == 1, d_mid, jnp.where(row16 == 2, d_lo, 0.0)))
        dec_col = _dot_tn(dec3.astype(BF16), ones16)
        st_ref[0, cs0:cs0 + gw_cols, :] = s_in * dec_col + new_states

        y = (y + dsk_ref[:, cs0:cs0 + gw_cols] * xs)
        zz = z_ref[0, :, cs0:cs0 + gw_cols]
        y = y * (zz * _sigmoid(zz))
        ms = jnp.mean(y * y, axis=-1, keepdims=True)
        y = y * lax.rsqrt(ms + RMS_EPS) * gw_ref[:, cs0:cs0 + gw_cols]
        y_ref[0, :, cs0:cs0 + gw_cols] = y.astype(y_ref.dtype)


def ssd_chunk(z, xbc, conv_state, dt_e, acum_e, acum, ssm_state, conv_w, conv_b, d_skip, gnorm_w, *, q_len):
    nb, l, inner = z.shape
    conv_dim = xbc.shape[2]
    nh = inner // SSD_HEAD_DIM
    n = SSD_STATE
    assert l % q_len == 0
    n_chunks = l // q_len
    assert q_len == LANES or (q_len < 16 and n_chunks == 1)
    y_dtype = BF16 if q_len >= 16 else F32
    dsk = jnp.repeat(d_skip.astype(F32), SSD_HEAD_DIM).reshape(1, inner)

    def seq_chunk(b, c):
        return (b, c, 0)

    def seq_only(b, c):
        return (b, 0, 0)

    def const2(b, c):
        return (0, 0)

    in_specs = [pl.BlockSpec((1, q_len, inner), seq_chunk), pl.BlockSpec((1, q_len, conv_dim), seq_chunk)]
    args = [z, xbc]
    if n_chunks > 1:
        per = q_len // HALO
        in_specs.append(pl.BlockSpec((1, HALO, conv_dim), lambda b, c: (b, jnp.maximum(c * per - 1, 0), 0)))
        args.append(xbc)
    in_specs += [
        pl.BlockSpec((1, SSD_CONV - 1, conv_dim), seq_only),
        pl.BlockSpec((1, q_len, inner), seq_chunk),
        pl.BlockSpec((1, q_len, inner), seq_chunk),
        pl.BlockSpec((1, q_len, LANES), seq_chunk),
        pl.BlockSpec((1, inner, n), seq_only),
        pl.BlockSpec((SSD_CONV, conv_dim), const2),
        pl.BlockSpec((1, conv_dim), const2),
        pl.BlockSpec((1, inner), const2),
        pl.BlockSpec((1, inner), const2),
    ]
    args += [conv_state, dt_e, acum_e, acum, ssm_state, conv_w, conv_b.reshape(1, conv_dim), dsk,
             gnorm_w.reshape(1, inner)]
    qp = -(-q_len // SUBLANES) * SUBLANES
    blocks = (4 * _nbytes((q_len, inner), F32) + _nbytes((q_len, conv_dim), F32) + _nbytes((HALO, conv_dim), F32)
              + 2 * _nbytes((inner, n), F32) + _nbytes((q_len, LANES), F32))
    scratch = _nbytes((HALO + qp, conv_dim), F32)
    return pl.pallas_call(
        functools.partial(_ssd_chunk_kernel, q_len=q_len, n_chunks=n_chunks, nh=nh),
        grid=(nb, n_chunks),
        in_specs=in_specs,
        out_specs=[pl.BlockSpec((1, q_len, inner), seq_chunk), pl.BlockSpec((1, inner, n), seq_only)],
        out_shape=[jax.ShapeDtypeStruct((nb, l, inner), y_dtype), jax.ShapeDtypeStruct((nb, inner, n), F32)],
        scratch_shapes=[pltpu.VMEM((HALO + qp, conv_dim), F32)],
        compiler_params=pltpu.CompilerParams(
            dimension_semantics=("arbitrary", "arbitrary"),
            vmem_limit_bytes=_vmem_limit(blocks, scratch, 6 * _nbytes((max(q_len, SUBLANES), conv_dim), F32)),
        ),
        name="ssd_chunk",
    )(*args)


def _row_block(m, target):
    return target if m % target == 0 else m


def _trunk(x, paged, conv_states, ssm_states, p):
    nb, l, dm = x.shape
    m = nb * l
    xf = x.reshape(m, dm)
    bm = _row_block(m, 1024)
    bm_out = _row_block(m, 512)
    ks, vs, convs, ssms = [], [], [], []
    depth = p["depth"]
    for i in range(depth):
        j = i // 2
        if i % 2 == 0:
            w_in, w_out = p["w_in_sb"][j], p["w_out_sb"][j]
            width = w_out.shape[0]
            scale = SB_HEAD_DIM ** -0.5
            proj = functools.partial(norm_matmul, xf, p["norm_sb"][j], w_in, n_cols=width, bm=bm, bn=512)
            gate = proj(col_start=3 * width, out_dtypes=(F32,))
            if paged is None:
                k, k16 = proj(col_start=width, out_dtypes=(F32, BF16))
                v, v16 = proj(col_start=2 * width, out_dtypes=(F32, BF16))
                q = proj(col_start=0, out_dtypes=(BF16,), out_scale=scale * LOG2E)
                og = jnp.concatenate([
                    sb_prompt_attn(q[b * l:(b + 1) * l], k16[b * l:(b + 1) * l], v16[b * l:(b + 1) * l],
                                   gate[b * l:(b + 1) * l], p["sb_bias"][j], tq=_row_block(l, 1024), tk=256)
                    for b in range(nb)], axis=0)
            else:
                cache_k, cache_v, page_table, n_phys, page = paged
                k = proj(col_start=width, out_dtypes=(F32,))
                v = proj(col_start=2 * width, out_dtypes=(F32,))
                q = proj(col_start=0, out_dtypes=(F32,), out_scale=scale)
                og = sb_paged_attn(q.reshape(nb, l, width), k.reshape(nb, l, width), v.reshape(nb, l, width),
                                   gate.reshape(nb, l, width), p["sb_bias"][j], cache_k, cache_v, page_table,
                                   j * n_phys, page=page,
                                   pps=max(c for c in (8, 4, 2, 1) if page_table.shape[1] % c == 0)).reshape(m, width)
            xf = matmul_res(og, w_out, xf, bm=bm_out, bk=width)
            nh = width // SB_HEAD_DIM
            ks.append(k.reshape(nb, l, nh, SB_HEAD_DIM))
            vs.append(v.reshape(nb, l, nh, SB_HEAD_DIM))
        else:
            w_in, w_dt, w_out = p["w_in_ssd"][j], p["w_dt_ssd"][j], p["w_out_ssd"][j]
            inner = w_out.shape[0]
            conv_dim = p["conv_w"].shape[2]
            nh = inner // SSD_HEAD_DIM
            proj = functools.partial(norm_matmul, xf, p["norm_ssd"][j], bm=bm)
            z = proj(w_in, col_start=0, n_cols=inner, out_dtypes=(F32,), bn=512)
            xbc = proj(w_in, col_start=inner, n_cols=conv_dim, out_dtypes=(F32,), bn=512)
            dt_raw = proj(w_dt, col_start=0, n_cols=LANES, out_dtypes=(F32,), bn=LANES)
            q_len = SSD_CHUNK if l % SSD_CHUNK == 0 else l
            dt_e, acum_e, acum = ssd_prep(dt_raw, p["dt_bias"][j], p["a_log"][j], chunk=q_len,
                                          rows=_row_block(m, LANES))
            y, new_ssm = ssd_chunk(
                z.reshape(nb, l, inner), xbc.reshape(nb, l, conv_dim), conv_states[j],
                dt_e.reshape(nb, l, inner), acum_e.reshape(nb, l, inner), acum.reshape(nb, l, LANES),
                ssm_states[j].reshape(nb, inner, SSD_STATE), p["conv_w"][j], p["conv_b"][j], p["d_skip"][j],
                p["gnorm_w"][j], q_len=q_len)
            last = i == depth - 1
            xf = matmul_res(y.reshape(m, inner), w_out, xf, p["norm_f"] if last else None,
                            bm=_row_block(m, 256), bk=inner)
            xbc3 = xbc.reshape(nb, l, conv_dim)
            nprev = SSD_CONV - 1
            if l >= nprev:
                convs.append(xbc3[:, l - nprev:])
            else:
                convs.append(jnp.concatenate([conv_states[j].astype(F32), xbc3], axis=1)[:, l:])
            ssms.append(new_ssm.reshape(nb, nh, SSD_HEAD_DIM, SSD_STATE))
    if depth % 2 == 1:
        raise NotImplementedError("final norm is fused into the last SSD output projection")
    return xf.reshape(nb, l, dm), jnp.stack(ks), jnp.stack(vs), jnp.stack(convs), jnp.stack(ssms)


def kernel(x_prompt, x_sample, cache_k, cache_v, state_conv, state_ssm, page_table, norm_sb, w_in_sb, w_out_sb,
           sb_bias, norm_ssd, w_in_ssd, conv_w, conv_b, dt_bias, a_log, d_skip, gnorm_w, w_out_ssd, norm_f):
    n_sb = w_in_sb.shape[0]
    n_ssd = w_in_ssd.shape[0]
    inner = w_out_ssd.shape[1]
    conv_dim = conv_w.shape[2]
    nh_ssd = dt_bias.shape[1]
    dt_cols = w_in_ssd[:, :, inner + conv_dim:inner + conv_dim + nh_ssd]
    w_dt = jnp.zeros((n_ssd, w_in_ssd.shape[1], LANES), F32).at[:, :, :nh_ssd].set(dt_cols)
    p = dict(
        depth=n_sb + n_ssd,
        norm_sb=norm_sb, w_in_sb=w_in_sb.astype(BF16), w_out_sb=w_out_sb.astype(BF16), sb_bias=sb_bias,
        norm_ssd=norm_ssd, w_in_ssd=w_in_ssd[:, :, :inner + conv_dim].astype(BF16), w_dt_ssd=w_dt.astype(BF16),
        w_out_ssd=w_out_ssd.astype(BF16), conv_w=conv_w, conv_b=conv_b, dt_bias=dt_bias, a_log=a_log,
        d_skip=d_skip, gnorm_w=gnorm_w, norm_f=norm_f,
    )
    nbp = x_prompt.shape[0]
    zero_conv = jnp.zeros((n_ssd, nbp, SSD_CONV - 1, conv_dim), x_prompt.dtype)
    zero_ssm = jnp.zeros((n_ssd, nbp, nh_ssd, SSD_HEAD_DIM, SSD_STATE), x_prompt.dtype)
    y_p, k_p, v_p, conv_p, ssm_p = _trunk(x_prompt, None, zero_conv, zero_ssm, p)

    n_phys, page = cache_k.shape[1], cache_k.shape[2]
    paged = (cache_k.reshape(-1, cache_k.shape[4]), cache_v.reshape(-1, cache_v.shape[4]), page_table, n_phys, page)
    y_s, k_s, v_s, conv_s, ssm_s = _trunk(x_sample, paged, state_conv, state_ssm, p)
    return (y_p, y_s, k_p, v_p, conv_p, ssm_p, k_s, v_s, conv_s, ssm_s)
```

```python
import functools

import jax
import jax.numpy as jnp
from jax import lax
from jax.experimental import pallas as pl
from jax.experimental.pallas import tpu as pltpu

F32 = jnp.float32
BF16 = jnp.bfloat16
RMS_EPS = 1e-6

SB_HEAD_DIM = 128
SSD_HEAD_DIM = 64
SSD_GROUPS = 8
SSD_STATE = 128
SSD_CONV = 4
SSD_CHUNK = 128

LANES = 128
SUBLANES = 8
VMEM_PHYSICAL_BYTES = 64 * 1024 * 1024
VMEM_LIMIT_CAP_BYTES = 56 * 1024 * 1024


def _vmem_limit(block_bytes, scratch_bytes=0, temp_bytes=0):
    est = 2 * block_bytes + scratch_bytes + temp_bytes
    return int(min(max(est, 16 * 1024 * 1024), VMEM_LIMIT_CAP_BYTES))


def _nbytes(shape, dtype):
    n = 1
    for s in shape:
        n *= s
    return n * jnp.dtype(dtype).itemsize


def _sigmoid(x):
    return 1.0 / (1.0 + jnp.exp(-x))


def _softplus(x):
    return jnp.maximum(x, 0.0) + jnp.log(1.0 + jnp.exp(-jnp.abs(x)))


def _split2(x):
    hi = x.astype(BF16)
    lo = (x - hi.astype(F32)).astype(BF16)
    return hi, lo


def _split3(x):
    hi = x.astype(BF16)
    r1 = x - hi.astype(F32)
    mid = r1.astype(BF16)
    lo = (r1 - mid.astype(F32)).astype(BF16)
    return hi, mid, lo


def _dot(a, b):
    return jnp.dot(a, b, preferred_element_type=F32)


def _dot_nt(a, b):
    return lax.dot_general(a, b, (((1,), (1,)), ((), ())), preferred_element_type=F32)


def _dot_tn(a, b):
    return lax.dot_general(a, b, (((0,), (0,)), ((), ())), preferred_element_type=F32)


def _norm_matmul_kernel(x_ref, g_ref, w_ref, *rest, out_scale):
    o_refs, xn_ref = rest[:-1], rest[-1]

    @pl.when(pl.program_id(1) == 0)
    def _():
        x = x_ref[...]
        ms = jnp.mean(x * x, axis=-1, keepdims=True)
        xn_ref[...] = (x * lax.rsqrt(ms + RMS_EPS) * g_ref[...]).astype(BF16)

    acc = _dot(xn_ref[...], w_ref[...])
    if out_scale != 1.0:
        acc = acc * out_scale
    for o_ref in o_refs:
        o_ref[...] = acc.astype(o_ref.dtype)


def norm_matmul(x, g, w, *, col_start, n_cols, out_dtypes, out_scale=1.0, bm, bn):
    m, k = x.shape
    assert m % bm == 0 and n_cols % bn == 0 and col_start % bn == 0
    off = col_start // bn
    blocks = _nbytes((bm, k), F32) + _nbytes((k, bn), BF16) + sum(_nbytes((bm, bn), dt) for dt in out_dtypes)
    outs = pl.pallas_call(
        functools.partial(_norm_matmul_kernel, out_scale=out_scale),
        grid=(m // bm, n_cols // bn),
        in_specs=[
            pl.BlockSpec((bm, k), lambda i, j: (i, 0)),
            pl.BlockSpec((1, k), lambda i, j: (0, 0)),
            pl.BlockSpec((k, bn), lambda i, j: (0, j + off)),
        ],
        out_specs=[pl.BlockSpec((bm, bn), lambda i, j: (i, j)) for _ in out_dtypes],
        out_shape=[jax.ShapeDtypeStruct((m, n_cols), dt) for dt in out_dtypes],
        scratch_shapes=[pltpu.VMEM((bm, k), BF16)],
        compiler_params=pltpu.CompilerParams(
            dimension_semantics=("arbitrary", "arbitrary"),
            vmem_limit_bytes=_vmem_limit(blocks, _nbytes((bm, k), BF16), 2 * _nbytes((bm, k), F32)),
        ),
        name="norm_matmul",
    )(x, g.reshape(1, k), w)
    return outs[0] if len(outs) == 1 else tuple(outs)


def _matmul_res_kernel(a_ref, w_ref, r_ref, g_ref, o_ref, *, final_norm):
    kk = pl.program_id(1)

    @pl.when(kk == 0)
    def _():
        o_ref[...] = r_ref[...]

    o_ref[...] += _dot(a_ref[...].astype(BF16), w_ref[...])

    if final_norm:
        @pl.when(kk == pl.num_programs(1) - 1)
        def _():
            x = o_ref[...]
            ms = jnp.mean(x * x, axis=-1, keepdims=True)
            o_ref[...] = x * lax.rsqrt(ms + RMS_EPS) * g_ref[...]


def matmul_res(a, w, res, g=None, *, bm, bk):
    m, k = a.shape
    n = w.shape[1]
    assert m % bm == 0 and k % bk == 0
    final_norm = g is not None
    if g is None:
        g = jnp.ones((n,), F32)
    blocks = _nbytes((bm, bk), a.dtype) + _nbytes((bk, n), BF16) + 2 * _nbytes((bm, n), F32)
    return pl.pallas_call(
        functools.partial(_matmul_res_kernel, final_norm=final_norm),
        grid=(m // bm, k // bk),
        in_specs=[
            pl.BlockSpec((bm, bk), lambda i, j: (i, j)),
            pl.BlockSpec((bk, n), lambda i, j: (j, 0)),
            pl.BlockSpec((bm, n), lambda i, j: (i, 0)),
            pl.BlockSpec((1, n), lambda i, j: (0, 0)),
        ],
        out_specs=pl.BlockSpec((bm, n), lambda i, j: (i, 0)),
        out_shape=jax.ShapeDtypeStruct((m, n), F32),
        compiler_params=pltpu.CompilerParams(
            dimension_semantics=("arbitrary", "arbitrary"),
            vmem_limit_bytes=_vmem_limit(blocks, 0, 2 * _nbytes((bm, n), F32)),
        ),
        name="matmul_res",
    )(a, w, res, g.reshape(1, n))


def _neg_softplus(z):
    return -_softplus(z)


LOG2E = 1.4426950408889634
MASKED_LOG = -1e30


def _sb_prompt_kernel(bias_ref, q_ref, k_ref, v_ref, gate_ref, o_ref, lb_ref, lk_ref, lk0_ref, carry_ref, acc_ref,
                      *, tq, tk):
    h = pl.program_id(0)
    qi = pl.program_id(1)
    bias2 = bias_ref[h] * LOG2E
    nd = tq // tk

    def stage1(kb, slot, masked, r0=0):
        start = pl.multiple_of(kb * tk, tk)
        z = _dot_nt(q_ref[r0:, :], k_ref[pl.ds(start, tk), :]) + bias2
        lb = jnp.minimum(z, 0.0) - jnp.log2(1.0 + jnp.exp2(-jnp.abs(z)))
        lk = lb - z
        if masked:
            t_pos = qi * tq + r0 + lax.broadcasted_iota(jnp.int32, (tq - r0, tk), 0)
            s_pos = kb * tk + lax.broadcasted_iota(jnp.int32, (tq - r0, tk), 1)
            lk = jnp.where(s_pos < t_pos, lk, 0.0)
            lb = jnp.where(s_pos < t_pos, lb, MASKED_LOG)
        lb_ref[slot, r0:, :] = lb
        lk_ref[slot, r0:, :] = lk.astype(BF16)
        lk0_ref[slot, r0:, :] = lk[:, 0:1]

    def stage2(kb, slot, r0=0):
        start = pl.multiple_of(kb * tk, tk)
        r = lax.broadcasted_iota(jnp.int32, (tk, tk), 0)
        c = lax.broadcasted_iota(jnp.int32, (tk, tk), 1)
        tri = jnp.where(r > c, 1.0, 0.0).astype(BF16)
        mm = _dot(lk_ref[slot, r0:, :], tri)
        w = jnp.exp2(lb_ref[slot, r0:, :] + mm)
        carry = carry_ref[r0:, :]
        acc_ref[r0:, :] += _dot(w.astype(BF16), v_ref[pl.ds(start, tk), :]) * jnp.exp2(carry)
        carry_ref[r0:, :] = carry + mm[:, 0:1] + lk0_ref[slot, r0:, :]

    carry_ref[...] = jnp.zeros_like(carry_ref)
    acc_ref[...] = jnp.zeros_like(acc_ref)
    kb_top = qi * nd + nd - 1
    stage1(kb_top, 0, True, (nd - 1) * tk)
    for dd in range(1, nd):
        stage1(kb_top - dd, dd % 2, True, (nd - 1 - dd) * tk)
        stage2(kb_top - dd + 1, (dd - 1) % 2, (nd - dd) * tk)
    pending = (nd - 1) % 2

    def body(i, _):
        kb = qi * nd - 1 - 2 * i
        stage1(kb, 1 - pending, False)
        stage2(kb + 1, pending)
        stage1(kb - 1, pending, False)
        stage2(kb, 1 - pending)
        return 0

    lax.fori_loop(0, (qi * nd) // 2, body, 0)
    stage2(0, pending)
    g = gate_ref[...]
    o_ref[...] = (acc_ref[...] * (g * _sigmoid(g))).astype(o_ref.dtype)


def sb_prompt_attn(q, k, v, gate, bias, *, tq, tk):
    l, width = q.shape
    d = SB_HEAD_DIM
    nh = width // d
    assert l % tq == 0 and tq % (2 * tk) == 0
    blocks = _nbytes((tq, d), BF16) * 2 + 2 * _nbytes((l, d), BF16) + _nbytes((tq, d), F32)
    scratch_shapes = [
        pltpu.VMEM((2, tq, tk), F32),
        pltpu.VMEM((2, tq, tk), BF16),
        pltpu.VMEM((2, tq, 1), F32),
        pltpu.VMEM((tq, 1), F32),
        pltpu.VMEM((tq, d), F32),
    ]
    scratch = 2 * _nbytes((tq, tk), F32) + 2 * _nbytes((tq, tk), BF16) + 4 * _nbytes((tq, LANES), F32)
    return pl.pallas_call(
        functools.partial(_sb_prompt_kernel, tq=tq, tk=tk),
        grid=(nh, l // tq),
        in_specs=[
            pl.BlockSpec(memory_space=pltpu.SMEM),
            pl.BlockSpec((tq, d), lambda h, i: (i, h)),
            pl.BlockSpec((l, d), lambda h, i: (0, h)),
            pl.BlockSpec((l, d), lambda h, i: (0, h)),
            pl.BlockSpec((tq, d), lambda h, i: (i, h)),
        ],
        out_specs=pl.BlockSpec((tq, d), lambda h, i: (i, h)),
        out_shape=jax.ShapeDtypeStruct((l, width), BF16),
        scratch_shapes=scratch_shapes,
        compiler_params=pltpu.CompilerParams(
            dimension_semantics=("arbitrary", "arbitrary"),
            vmem_limit_bytes=_vmem_limit(blocks, scratch, 12 * _nbytes((tq, tk), F32)),
        ),
        name="sb_prompt_attn",
    )(bias, q, k, v, gate)


NEW_ROWS = 16


def _sb_paged_kernel(pt_ref, q_ref, kn_ref, vn_ref, gate_ref, bias_ref, *rest, nq, nh, page, pps):
    kp_refs, vp_refs = rest[:pps], rest[pps:2 * pps]
    o_ref, qrows_ref, kscr_ref, vscr_ref, carry_ref, acc_ref = rest[2 * pps:]
    b = pl.program_id(0)
    p = pl.program_id(1)
    d = SB_HEAD_DIM
    width = nh * d
    ncol = nh * nq

    def suffix_tri(rows):
        r = lax.broadcasted_iota(jnp.int32, (rows, 2 * rows), 0)
        c = lax.broadcasted_iota(jnp.int32, (rows, 2 * rows), 1)
        return jnp.where(jnp.where(c >= rows, c - rows, c) >= r, 1.0, 0.0).astype(BF16)

    def blocks_step(k16, v16, rows, mask):
        n = k16.shape[0] // rows
        tri2 = suffix_tri(rows)
        z = _dot_nt(k16, qrows_ref[...]) + bias_ref[...]
        lk = _neg_softplus(z)
        if mask is not None:
            lk = jnp.where(mask, lk, 0.0)
        hi, lo = _split2(lk)
        carry = carry_ref[...]
        ws = []
        for i in range(n):
            sl = slice(i * rows, (i + 1) * rows)
            cs = _dot(tri2, jnp.concatenate([hi[sl], lo[sl]], axis=0)) + carry
            ws.append(jnp.exp(z[sl] + cs))
            carry = cs[0:1, :]
        w = ws[0] if n == 1 else jnp.concatenate(ws, axis=0)
        if mask is not None:
            w = jnp.where(mask, w, 0.0)
        acc_ref[...] += _dot_tn(w.astype(BF16), v16)
        carry_ref[...] = carry

    def load_pages(refs):
        pages = [jnp.concatenate([ref[pl.ds(hh, page, stride=nh), :] for hh in range(nh)], axis=1).astype(BF16)
                 for ref in refs]
        return pages[0] if pps == 1 else jnp.concatenate(pages, axis=0)

    @pl.when((b == 0) & (p == 0))
    def _():
        kscr_ref[...] = jnp.zeros_like(kscr_ref)
        vscr_ref[...] = jnp.zeros_like(vscr_ref)

    @pl.when(p == 0)
    def _():
        qt = jnp.concatenate([q_ref[0]] * nh, axis=0)
        rh = lax.broadcasted_iota(jnp.int32, (ncol, width), 0) // nq
        ch = lax.broadcasted_iota(jnp.int32, (ncol, width), 1) // d
        qrows_ref[...] = jnp.where(rh == ch, qt, 0.0).astype(BF16)
        carry_ref[...] = jnp.zeros_like(carry_ref)
        acc_ref[...] = jnp.zeros_like(acc_ref)
        kscr_ref[0:nq, :] = kn_ref[0]
        vscr_ref[0:nq, :] = vn_ref[0]
        j = lax.broadcasted_iota(jnp.int32, (NEW_ROWS, ncol), 0)
        i = lax.broadcasted_iota(jnp.int32, (NEW_ROWS, ncol), 1) % nq
        blocks_step(kscr_ref[...].astype(BF16), vscr_ref[...].astype(BF16), NEW_ROWS, j < i)

    blocks_step(load_pages(kp_refs), load_pages(vp_refs), page, None)

    @pl.when(p == pl.num_programs(1) - 1)
    def _():
        acc = acc_ref[...]
        rh = lax.broadcasted_iota(jnp.int32, (ncol, d), 0) // nq
        o = jnp.zeros((ncol, d), F32)
        for hh in range(nh):
            o = o + jnp.where(rh == hh, acc[:, hh * d:(hh + 1) * d], 0.0)
        o = jnp.concatenate([o[hh * nq:(hh + 1) * nq, :] for hh in range(nh)], axis=1)
        g = gate_ref[0]
        o_ref[0] = o * (g * _sigmoid(g))


def sb_paged_attn(q, k_new, v_new, gate, bias, cache_k, cache_v, page_table, page_base, *, page, pps):
    nb, nq, width = q.shape
    d = SB_HEAD_DIM
    nh = width // d
    n_pages = page_table.shape[1]
    assert n_pages % pps == 0 and nq <= NEW_ROWS
    ncol = nh * nq
    bias_cols = jnp.repeat(bias.astype(F32), nq).reshape(1, ncol)
    pt = page_table.reshape(-1).astype(jnp.int32) + page_base

    def tok_map(b, p, pt_ref):
        return (b, 0, 0)

    def page_map(slot):
        return lambda b, p, pt_ref: (pt_ref[b * n_pages + (n_pages - 1 - p * pps - slot)], 0)

    page_specs = [pl.BlockSpec((page * nh, d), page_map(s)) for s in range(pps)]
    blocks = 5 * _nbytes((nq, width), F32) + 2 * pps * _nbytes((page, width), F32)
    scratch = (_nbytes((ncol, width), BF16) + 2 * _nbytes((NEW_ROWS, width), F32) + _nbytes((ncol, width), F32))
    grid_spec = pltpu.PrefetchScalarGridSpec(
        num_scalar_prefetch=1,
        grid=(nb, n_pages // pps),
        in_specs=[
            pl.BlockSpec((1, nq, width), tok_map),
            pl.BlockSpec((1, nq, width), tok_map),
            pl.BlockSpec((1, nq, width), tok_map),
            pl.BlockSpec((1, nq, width), tok_map),
            pl.BlockSpec((1, ncol), lambda b, p, pt_ref: (0, 0)),
        ] + page_specs + page_specs,
        out_specs=pl.BlockSpec((1, nq, width), tok_map),
        scratch_shapes=[
            pltpu.VMEM((ncol, width), BF16),
            pltpu.VMEM((NEW_ROWS, width), F32),
            pltpu.VMEM((NEW_ROWS, width), F32),
            pltpu.VMEM((1, ncol), F32),
            pltpu.VMEM((ncol, width), F32),
        ],
    )
    return pl.pallas_call(
        functools.partial(_sb_paged_kernel, nq=nq, nh=nh, page=page, pps=pps),
        grid_spec=grid_spec,
        out_shape=jax.ShapeDtypeStruct((nb, nq, width), F32),
        compiler_params=pltpu.CompilerParams(
            dimension_semantics=("arbitrary", "arbitrary"),
            vmem_limit_bytes=_vmem_limit(blocks, scratch, 3 * pps * _nbytes((page, width), F32)),
        ),
        name="sb_paged_attn",
    )(pt, q, k_new, v_new, gate, bias_cols, *([cache_k] * pps), *([cache_v] * pps))


def _ssd_prep_kernel(dtraw_ref, dtb_ref, alog_ref, e3_ref, dte_ref, ace_ref, ac_ref, *, chunk):
    rows = dtraw_ref.shape[0]
    dt = _softplus(dtraw_ref[...] + dtb_ref[...])
    a = -jnp.exp(alog_ref[...])
    da = dt * a
    q = lax.broadcasted_iota(jnp.int32, (rows, 3 * rows), 0)
    s = lax.broadcasted_iota(jnp.int32, (rows, 3 * rows), 1) % rows
    tri3 = jnp.where((s <= q) & (s // chunk == q // chunk), 1.0, 0.0).astype(BF16)
    acum = _dot(tri3, jnp.concatenate(_split3(da), axis=0))
    ac_ref[...] = acum
    dte_ref[...] = _dot(jnp.concatenate(_split3(dt), axis=1), e3_ref[...])
    ace_ref[...] = _dot(jnp.concatenate(_split3(acum), axis=1), e3_ref[...])


def ssd_prep(dt_raw, dt_bias, a_log, *, chunk, rows):
    m = dt_raw.shape[0]
    nh = dt_bias.shape[0]
    inner = nh * SSD_HEAD_DIM
    assert m % rows == 0 and rows % chunk == 0 and nh <= LANES

    def pad(v):
        return jnp.zeros((1, LANES), F32).at[0, :nh].set(v.astype(F32))

    lane = jnp.arange(3 * LANES) % LANES
    e3 = (lane[:, None] == (jnp.arange(inner) // SSD_HEAD_DIM)[None, :]).astype(BF16)
    blocks = 2 * _nbytes((rows, LANES), F32) + _nbytes((3 * LANES, inner), BF16) + 2 * _nbytes((rows, inner), F32)
    return pl.pallas_call(
        functools.partial(_ssd_prep_kernel, chunk=chunk),
        grid=(m // rows,),
        in_specs=[
            pl.BlockSpec((rows, LANES), lambda i: (i, 0)),
            pl.BlockSpec((1, LANES), lambda i: (0, 0)),
            pl.BlockSpec((1, LANES), lambda i: (0, 0)),
            pl.BlockSpec((3 * LANES, inner), lambda i: (0, 0)),
        ],
        out_specs=[
            pl.BlockSpec((rows, inner), lambda i: (i, 0)),
            pl.BlockSpec((rows, inner), lambda i: (i, 0)),
            pl.BlockSpec((rows, LANES), lambda i: (i, 0)),
        ],
        out_shape=[
            jax.ShapeDtypeStruct((m, inner), F32),
            jax.ShapeDtypeStruct((m, inner), F32),
            jax.ShapeDtypeStruct((m, LANES), F32),
        ],
        compiler_params=pltpu.CompilerParams(
            dimension_semantics=("arbitrary",),
            vmem_limit_bytes=_vmem_limit(blocks, 0, 2 * _nbytes((rows, inner), F32)),
        ),
        name="ssd_prep",
    )(dt_raw, pad(dt_bias), pad(a_log), e3)


HALO = SUBLANES


def _ssd_chunk_kernel(*refs, q_len, n_chunks, nh):
    if n_chunks > 1:
        (z_ref, xbc_ref, halo_ref, cs_ref, dte_ref, ace_ref, ac_ref, s0_ref, cw_ref, cb_ref, dsk_ref, gw_ref,
         y_ref, st_ref, xp_ref) = refs
    else:
        (z_ref, xbc_ref, cs_ref, dte_ref, ace_ref, ac_ref, s0_ref, cw_ref, cb_ref, dsk_ref, gw_ref,
         y_ref, st_ref, xp_ref) = refs
        halo_ref = None
    ci = pl.program_id(1)
    p = SSD_HEAD_DIM
    n = SSD_STATE
    inner = nh * p
    gw_cols = inner // SSD_GROUPS
    hg = nh // SSD_GROUPS
    nprev = SSD_CONV - 1

    @pl.when(ci == 0)
    def _():
        st_ref[...] = s0_ref[...]

    xp_ref[HALO:HALO + q_len, :] = xbc_ref[0]
    if halo_ref is None:
        xp_ref[HALO - nprev:HALO, :] = cs_ref[0]
    else:
        @pl.when(ci == 0)
        def _():
            xp_ref[HALO - nprev:HALO, :] = cs_ref[0]

        @pl.when(ci > 0)
        def _():
            xp_ref[HALO - nprev:HALO, :] = halo_ref[0, HALO - nprev:HALO, :]

    conv = cb_ref[...]
    for j in range(SSD_CONV):
        conv = conv + xp_ref[HALO - nprev + j:HALO - nprev + j + q_len, :] * cw_ref[j:j + 1, :]
    xc = conv * _sigmoid(conv)

    row = lax.broadcasted_iota(jnp.int32, (q_len, q_len), 0)
    col = lax.broadcasted_iota(jnp.int32, (q_len, q_len), 1)
    causal = row >= col
    lane = lax.broadcasted_iota(jnp.int32, (q_len, LANES), 1)
    if q_len >= LANES:
        act = ac_ref[0].T
    ones16 = jnp.where(lax.broadcasted_iota(jnp.int32, (16, n), 0) < 3, 1.0, 0.0).astype(BF16)
    row16 = lax.broadcasted_iota(jnp.int32, (16, gw_cols), 0)

    for g in range(SSD_GROUPS):
        cs0 = g * gw_cols
        xs = xc[:, cs0:cs0 + gw_cols]
        bq = xc[:, inner + g * n:inner + (g + 1) * n]
        cq = xc[:, inner + SSD_GROUPS * n + g * n:inner + SSD_GROUPS * n + (g + 1) * n]
        dte = dte_ref[0, :, cs0:cs0 + gw_cols]
        ace = ace_ref[0, :, cs0:cs0 + gw_cols]
        xdt = xs * dte
        bq16 = bq.astype(BF16)
        cq16 = cq.astype(BF16)
        s_in = st_ref[0, cs0:cs0 + gw_cols, :]

        if q_len >= LANES:
            cb = _dot_nt(cq16, bq16)
            parts = []
            for pi in range(hg // 2):
                ms = []
                for hh in (2 * pi, 2 * pi + 1):
                    hd = g * hg + hh
                    seg = ac_ref[0, :, hd:hd + 1] - act[hd:hd + 1, :]
                    ms.append(cb * jnp.where(causal, jnp.exp(seg), 0.0))
                lhs = jnp.concatenate(ms, axis=1).astype(BF16)
                xpair = xdt[:, pi * LANES:(pi + 1) * LANES]
                rhs = jnp.concatenate([jnp.where(lane < p, xpair, 0.0), jnp.where(lane >= p, xpair, 0.0)],
                                      axis=0).astype(BF16)
                parts.append(_dot(lhs, rhs))
            y = jnp.concatenate(parts, axis=1)
        else:
            rowq = lax.broadcasted_iota(jnp.int32, (q_len, gw_cols), 0)
            y = jnp.zeros((q_len, gw_cols), F32)
            for s in range(q_len):
                cb_s = jnp.sum(cq * bq[s:s + 1, :], axis=1, keepdims=True)
                term = cb_s * jnp.exp(ace - ace[s:s + 1, :]) * xdt[s:s + 1, :]
                y = y + jnp.where(rowq >= s, term, 0.0)

        y = y + _dot_nt(cq16, s_in.astype(BF16)) * jnp.exp(ace)
        ace_last = ace[q_len - 1:q_len, :]
        xw = xdt * jnp.exp(ace_last - ace)
        pad_rows = (-q_len) % 16
        if pad_rows:
            xw = jnp.concatenate([xw, jnp.zeros((pad_rows, gw_cols), F32)], axis=0)
            bqp = jnp.concatenate([bq, jnp.zeros((pad_rows, n), F32)], axis=0).astype(BF16)
        else:
            bqp = bq16
        new_states = _dot_tn(xw.astype(BF16), bqp)
        d_hi, d_mid, d_lo = (t.astype(F32) for t in _split3(jnp.exp(ace_last)))
        dec3 = jnp.where(row16 == 0, d_hi, jnp.where(row16 == 1, d_mid, jnp.where(row16 == 2, d_lo, 0.0)))
        dec_col = _dot_tn(dec3.astype(BF16), ones16)
        st_ref[0, cs0:cs0 + gw_cols, :] = s_in * dec_col + new_states

        y = (y + dsk_ref[:, cs0:cs0 + gw_cols] * xs)
        zz = z_ref[0, :, cs0:cs0 + gw_cols]
        y = y * (zz * _sigmoid(zz))
        ms = jnp.mean(y * y, axis=-1, keepdims=True)
        y = y * lax.rsqrt(ms + RMS_EPS) * gw_ref[:, cs0:cs0 + gw_cols]
        y_ref[0, :, cs0:cs0 + gw_cols] = y.astype(y_ref.dtype)


def ssd_chunk(z, xbc, conv_state, dt_e, acum_e, acum, ssm_state, conv_w, conv_b, d_skip, gnorm_w, *, q_len):
    nb, l, inner = z.shape
    conv_dim = xbc.shape[2]
    nh = inner // SSD_HEAD_DIM
    n = SSD_STATE
    assert l % q_len == 0
    n_chunks = l // q_len
    assert q_len == LANES or (q_len < 16 and n_chunks == 1)
    y_dtype = BF16 if q_len >= 16 else F32
    dsk = jnp.repeat(d_skip.astype(F32), SSD_HEAD_DIM).reshape(1, inner)

    def seq_chunk(b, c):
        return (b, c, 0)

    def seq_only(b, c):
        return (b, 0, 0)

    def const2(b, c):
        return (0, 0)

    in_specs = [pl.BlockSpec((1, q_len, inner), seq_chunk), pl.BlockSpec((1, q_len, conv_dim), seq_chunk)]
    args = [z, xbc]
    if n_chunks > 1:
        per = q_len // HALO
        in_specs.append(pl.BlockSpec((1, HALO, conv_dim), lambda b, c: (b, jnp.maximum(c * per - 1, 0), 0)))
        args.append(xbc)
    in_specs += [
        pl.BlockSpec((1, SSD_CONV - 1, conv_dim), seq_only),
        pl.BlockSpec((1, q_len, inner), seq_chunk),
        pl.BlockSpec((1, q_len, inner), seq_chunk),
        pl.BlockSpec((1, q_len, LANES), seq_chunk),
        pl.BlockSpec((1, inner, n), seq_only),
        pl.BlockSpec((SSD_CONV, conv_dim), const2),
        pl.BlockSpec((1, conv_dim), const2),
        pl.BlockSpec((1, inner), const2),
        pl.BlockSpec((1, inner), const2),
    ]
    args += [conv_state, dt_e, acum_e, acum, ssm_state, conv_w, conv_b.reshape(1, conv_dim), dsk,
             gnorm_w.reshape(1, inner)]
    qp = -(-q_len // SUBLANES) * SUBLANES
    blocks = (4 * _nbytes((q_len, inner), F32) + _nbytes((q_len, conv_dim), F32) + _nbytes((HALO, conv_dim), F32)
              + 2 * _nbytes((inner, n), F32) + _nbytes((q_len, LANES), F32))
    scratch = _nbytes((HALO + qp, conv_dim), F32)
    return pl.pallas_call(
        functools.partial(_ssd_chunk_kernel, q_len=q_len, n_chunks=n_chunks, nh=nh),
        grid=(nb, n_chunks),
        in_specs=in_specs,
        out_specs=[pl.BlockSpec((1, q_len, inner), seq_chunk), pl.BlockSpec((1, inner, n), seq_only)],
        out_shape=[jax.ShapeDtypeStruct((nb, l, inner), y_dtype), jax.ShapeDtypeStruct((nb, inner, n), F32)],
        scratch_shapes=[pltpu.VMEM((HALO + qp, conv_dim), F32)],
        compiler_params=pltpu.CompilerParams(
            dimension_semantics=("arbitrary", "arbitrary"),
            vmem_limit_bytes=_vmem_limit(blocks, scratch, 6 * _nbytes((max(q_len, SUBLANES), conv_dim), F32)),
        ),
        name="ssd_chunk",
    )(*args)


def _row_block(m, target):
    return target if m % target == 0 else m


def _trunk(x, paged, conv_states, ssm_states, p):
    nb, l, dm = x.shape
    m = nb * l
    xf = x.reshape(m, dm)
    bm = _row_block(m, 1024)
    bm_out = _row_block(m, 512)
    ks, vs, convs, ssms = [], [], [], []
    depth = p["depth"]
    for i in range(depth):
        j = i // 2
        if i % 2 == 0:
            w_in, w_out = p["w_in_sb"][j], p["w_out_sb"][j]
            width = w_out.shape[0]
            scale = SB_HEAD_DIM ** -0.5
            proj = functools.partial(norm_matmul, xf, p["norm_sb"][j], w_in, n_cols=width, bm=bm, bn=512)
            gate = proj(col_start=3 * width, out_dtypes=(F32,))
            if paged is None:
                k, k16 = proj(col_start=width, out_dtypes=(F32, BF16))
                v, v16 = proj(col_start=2 * width, out_dtypes=(F32, BF16))
                q = proj(col_start=0, out_dtypes=(BF16,), out_scale=scale * LOG2E)
                og = jnp.concatenate([
                    sb_prompt_attn(q[b * l:(b + 1) * l], k16[b * l:(b + 1) * l], v16[b * l:(b + 1) * l],
                                   gate[b * l:(b + 1) * l], p["sb_bias"][j], tq=_row_block(l, 2048), tk=256)
                    for b in range(nb)], axis=0)
            else:
                cache_k, cache_v, page_table, n_phys, page = paged
                k = proj(col_start=width, out_dtypes=(F32,))
                v = proj(col_start=2 * width, out_dtypes=(F32,))
                q = proj(col_start=0, out_dtypes=(F32,), out_scale=scale)
                og = sb_paged_attn(q.reshape(nb, l, width), k.reshape(nb, l, width), v.reshape(nb, l, width),
                                   gate.reshape(nb, l, width), p["sb_bias"][j], cache_k, cache_v, page_table,
                                   j * n_phys, page=page,
                                   pps=max(c for c in (8, 4, 2, 1) if page_table.shape[1] % c == 0)).reshape(m, width)
            xf = matmul_res(og, w_out, xf, bm=bm_out, bk=width)
            nh = width // SB_HEAD_DIM
            ks.append(k.reshape(nb, l, nh, SB_HEAD_DIM))
            vs.append(v.reshape(nb, l, nh, SB_HEAD_DIM))
        else:
            w_in, w_dt, w_out = p["w_in_ssd"][j], p["w_dt_ssd"][j], p["w_out_ssd"][j]
            inner = w_out.shape[0]
            conv_dim = p["conv_w"].shape[2]
            nh = inner // SSD_HEAD_DIM
            proj = functools.partial(norm_matmul, xf, p["norm_ssd"][j], bm=bm)
            z = proj(w_in, col_start=0, n_cols=inner, out_dtypes=(F32,), bn=512)
            xbc = proj(w_in, col_start=inner, n_cols=conv_dim, out_dtypes=(F32,), bn=512)
            dt_raw = proj(w_dt, col_start=0, n_cols=LANES, out_dtypes=(F32,), bn=LANES)
            q_len = SSD_CHUNK if l % SSD_CHUNK == 0 else l
            dt_e, acum_e, acum = ssd_prep(dt_raw, p["dt_bias"][j], p["a_log"][j], chunk=q_len,
                                          rows=_row_block(m, LANES))
            y, new_ssm = ssd_chunk(
                z.reshape(nb, l, inner), xbc.reshape(nb, l, conv_dim), conv_states[j],
                dt_e.reshape(nb, l, inner), acum_e.reshape(nb, l, inner), acum.reshape(nb, l, LANES),
                ssm_states[j].reshape(nb, inner, SSD_STATE), p["conv_w"][j], p["conv_b"][j], p["d_skip"][j],
                p["gnorm_w"][j], q_len=q_len)
            last = i == depth - 1
            xf = matmul_res(y.reshape(m, inner), w_out, xf, p["norm_f"] if last else None,
                            bm=_row_block(m, 256), bk=inner)
            xbc3 = xbc.reshape(nb, l, conv_dim)
            nprev = SSD_CONV - 1
            if l >= nprev:
                convs.append(xbc3[:, l - nprev:])
            else:
                convs.append(jnp.concatenate([conv_states[j].astype(F32), xbc3], axis=1)[:, l:])
            ssms.append(new_ssm.reshape(nb, nh, SSD_HEAD_DIM, SSD_STATE))
    if depth % 2 == 1:
        raise NotImplementedError("final norm is fused into the last SSD output projection")
    return xf.reshape(nb, l, dm), jnp.stack(ks), jnp.stack(vs), jnp.stack(convs), jnp.stack(ssms)


def kernel(x_prompt, x_sample, cache_k, cache_v, state_conv, state_ssm, page_table, norm_sb, w_in_sb, w_out_sb,
           sb_bias, norm_ssd, w_in_ssd, conv_w, conv_b, dt_bias, a_log, d_skip, gnorm_w, w_out_ssd, norm_f):
    n_sb = w_in_sb.shape[0]
    n_ssd = w_in_ssd.shape[0]
    inner = w_out_ssd.shape[1]
    conv_dim = conv_w.shape[2]
    nh_ssd = dt_bias.shape[1]
    dt_cols = w_in_ssd[:, :, inner + conv_dim:inner + conv_dim + nh_ssd]
    w_dt = jnp.zeros((n_ssd, w_in_ssd.shape[1], LANES), F32).at[:, :, :nh_ssd].set(dt_cols)
    p = dict(
        depth=n_sb + n_ssd,
        norm_sb=norm_sb, w_in_sb=w_in_sb.astype(BF16), w_out_sb=w_out_sb.astype(BF16), sb_bias=sb_bias,
        norm_ssd=norm_ssd, w_in_ssd=w_in_ssd[:, :, :inner + conv_dim].astype(BF16), w_dt_ssd=w_dt.astype(BF16),
        w_out_ssd=w_out_ssd.astype(BF16), conv_w=conv_w, conv_b=conv_b, dt_bias=dt_bias, a_log=a_log,
        d_skip=d_skip, gnorm_w=gnorm_w, norm_f=norm_f,
    )
    nbp = x_prompt.shape[0]
    zero_conv = jnp.zeros((n_ssd, nbp, SSD_CONV - 1, conv_dim), x_prompt.dtype)
    zero_ssm = jnp.zeros((n_ssd, nbp, nh_ssd, SSD_HEAD_DIM, SSD_STATE), x_prompt.dtype)
    y_p, k_p, v_p, conv_p, ssm_p = _trunk(x_prompt, None, zero_conv, zero_ssm, p)

    n_phys, page = cache_k.shape[1], cache_k.shape[2]
    paged = (cache_k.reshape(-1, cache_k.shape[4]), cache_v.reshape(-1, cache_v.shape[4]), page_table, n_phys, page)
    y_s, k_s, v_s, conv_s, ssm_s = _trunk(x_sample, paged, state_conv, state_ssm, p)
    return (y_p, y_s, k_p, v_p, conv_p, ssm_p, k_s, v_s, conv_s, ssm_s)
```
